```python
import math
import jax, jax.numpy as jnp
from jax import lax
import numpy as np

D_MODEL = 4096
BATCH = 4
SEQ = 2048
DEPTH = 2

MIX_WIDTH = D_MODEL
CONV_WIDTH = D_MODEL // 4
SB_WIDTH = D_MODEL // 2
SSM_WIDTH = D_MODEL // 4
CONV_TAPS = 31
SB_HEAD_DIM = 128
SB_HEADS = SB_WIDTH // SB_HEAD_DIM
SB_BLOCK = 128
SSM_GROUP = 16
SSM_GROUPS = SSM_WIDTH // SSM_GROUP
SSM_STATE = 64
DT_MIN = 1e-3
DT_MAX = 1e-1
MEM_LEN = 256
XA_HEADS = 4
XA_HEAD_DIM = D_MODEL // XA_HEADS
EPS = 1e-6

IN_SIZES = (CONV_WIDTH, CONV_WIDTH, CONV_WIDTH,
            SB_WIDTH, SB_WIDTH, SB_WIDTH, SB_WIDTH,
            SSM_WIDTH, SSM_WIDTH)
IN_WIDTH = 3 * CONV_WIDTH + 4 * SB_WIDTH + 2 * SSM_WIDTH

kernel_name = "hybrid_conv_stickbreak_s5_xattn"


def rmsnorm(x, g):
    xf = x.astype(jnp.float32)
    y = xf * lax.rsqrt(jnp.mean(xf * xf, axis=-1, keepdims=True) + EPS)
    return (y * g.astype(jnp.float32)).astype(x.dtype)


def layernorm(x, g, b):
    xf = x.astype(jnp.float32)
    mu = jnp.mean(xf, axis=-1, keepdims=True)
    xc = xf - mu
    y = xc * lax.rsqrt(jnp.mean(xc * xc, axis=-1, keepdims=True) + EPS)
    return (y * g.astype(jnp.float32) + b.astype(jnp.float32)).astype(x.dtype)


def conformer_conv(a_val, a_glu, conv_w, conv_b, ln_g, ln_b):
    u = a_val * jax.nn.sigmoid(a_glu)
    y = lax.conv_general_dilated(
        u, conv_w[:, None, :].astype(u.dtype), window_strides=(1,),
        padding=[(CONV_TAPS - 1, 0)],
        dimension_numbers=('NWC', 'WIO', 'NWC'),
        feature_group_count=CONV_WIDTH) + conv_b.astype(u.dtype)
    y = layernorm(y, ln_g, ln_b)
    return jax.nn.silu(y)


def stick_breaking_attention(q, k, v):
    bsz, seq, heads, hd = q.shape
    nblk = seq // SB_BLOCK
    scale = 1.0 / math.sqrt(hd)
    qb = q.reshape(bsz, nblk, SB_BLOCK, heads, hd).transpose(1, 0, 3, 2, 4)
    key_pos = jnp.arange(seq)

    def block(args):
        qi, i = args
        z = jnp.einsum('bhqd,bkhd->bhqk', qi, k,
                       preferred_element_type=jnp.float32) * scale
        q_pos = i * SB_BLOCK + jnp.arange(SB_BLOCK)
        before = key_pos[None, :] < q_pos[:, None]
        log_keep = jnp.where(before, jax.nn.log_sigmoid(-z), 0.0)
        later = lax.cumsum(log_keep, axis=3, reverse=True) - log_keep
        w = jnp.where(before, jnp.exp(jax.nn.log_sigmoid(z) + later), 0.0)
        return jnp.einsum('bhqk,bkhd->bqhd', w.astype(v.dtype), v)

    out = lax.map(block, (qb, jnp.arange(nblk)))
    return out.transpose(1, 0, 2, 3, 4).reshape(bsz, seq, heads * hd)


def s5_ssm(u, lam_re, lam_im, log_dt, b_re, b_im, c_re, c_im, d_skip, glu_w, glu_b):
    f32 = jnp.float32
    bsz, seq, _ = u.shape
    ug = u.astype(f32).reshape(bsz, seq, SSM_GROUPS, SSM_GROUP)
    lam = lax.complex(lam_re.astype(f32), lam_im.astype(f32))
    dt = jnp.exp(log_dt.astype(f32))[:, None]
    lam_bar = jnp.exp(lam * dt)
    b = lax.complex(b_re.astype(f32), b_im.astype(f32))
    b_bar = ((lam_bar - 1.0) / lam)[..., None] * b
    bu = jnp.einsum('gpc,bsgc->bsgp', b_bar, ug.astype(jnp.complex64))
    a = jnp.broadcast_to(lam_bar, bu.shape)

    def combine(e1, e2):
        a1, b1 = e1
        a2, b2 = e2
        return a1 * a2, a2 * b1 + b2

    _, h = lax.associative_scan(combine, (a, bu), axis=1)
    c = lax.complex(c_re.astype(f32), c_im.astype(f32))
    y = jnp.einsum('gcp,bsgp->bsgc', c, h).real + d_skip.astype(f32) * ug
    y = jax.nn.gelu(y.reshape(bsz, seq, SSM_WIDTH))
    y = y * jax.nn.sigmoid(y @ glu_w.astype(f32) + glu_b.astype(f32))
    return y.astype(u.dtype)


def memory_cross_attention(h, m, wq, wk, wv, wo):
    bsz, seq, _ = h.shape
    q = (h @ wq).reshape(bsz, seq, XA_HEADS, XA_HEAD_DIM)
    k = (m @ wk).reshape(bsz, m.shape[1], XA_HEADS, XA_HEAD_DIM)
    v = (m @ wv).reshape(bsz, m.shape[1], XA_HEADS, XA_HEAD_DIM)
    s = jnp.einsum('bqhd,bkhd->bhqk', q, k,
                   preferred_element_type=jnp.float32) / math.sqrt(XA_HEAD_DIM)
    p = jax.nn.softmax(s, axis=-1).astype(v.dtype)
    o = jnp.einsum('bhqk,bkhd->bqhd', p, v).reshape(bsz, seq, D_MODEL)
    return o @ wo


def setup_inputs(seed: int = 0) -> dict:
    key = jax.random.key(seed)
    ks = jax.random.split(key, 32)
    L, D = DEPTH, D_MODEL
    nrm = jax.random.normal
    f32 = jnp.float32

    def gain(k, n):
        return 1.0 + 0.02 * nrm(k, (L, n), f32)

    lam_im_base = math.pi * jnp.arange(SSM_STATE, dtype=f32)
    return {
        "x": nrm(ks[0], (BATCH, SEQ, D), f32),
        "mem": nrm(ks[1], (BATCH, MEM_LEN, D), f32),
        "pre_norm_g": gain(ks[2], D),
        "w_in": nrm(ks[3], (L, D, IN_WIDTH), f32) * D ** -0.5,
        "conv_w": nrm(ks[4], (L, CONV_TAPS, CONV_WIDTH), f32) * CONV_TAPS ** -0.5,
        "conv_b": 0.02 * nrm(ks[5], (L, CONV_WIDTH), f32),
        "conv_ln_g": gain(ks[6], CONV_WIDTH),
        "conv_ln_b": 0.02 * nrm(ks[7], (L, CONV_WIDTH), f32),
        "ssm_lambda_re": -0.5 + 0.01 * nrm(ks[8], (L, SSM_GROUPS, SSM_STATE), f32),
        "ssm_lambda_im": lam_im_base + 0.01 * nrm(ks[9], (L, SSM_GROUPS, SSM_STATE), f32),
        "ssm_log_dt": jax.random.uniform(ks[10], (L, SSM_GROUPS), f32,
                                         math.log(DT_MIN), math.log(DT_MAX)),
        "ssm_b_re": nrm(ks[11], (L, SSM_GROUPS, SSM_STATE, SSM_GROUP), f32) * (2 * SSM_GROUP) ** -0.5,
        "ssm_b_im": nrm(ks[12], (L, SSM_GROUPS, SSM_STATE, SSM_GROUP), f32) * (2 * SSM_GROUP) ** -0.5,
        "ssm_c_re": nrm(ks[13], (L, SSM_GROUPS, SSM_GROUP, SSM_STATE), f32) * (2 * SSM_STATE) ** -0.5,
        "ssm_c_im": nrm(ks[14], (L, SSM_GROUPS, SSM_GROUP, SSM_STATE), f32) * (2 * SSM_STATE) ** -0.5,
        "ssm_d": nrm(ks[15], (L, SSM_GROUPS, SSM_GROUP), f32),
        "ssm_glu_w": nrm(ks[16], (L, SSM_WIDTH, SSM_WIDTH), f32) * SSM_WIDTH ** -0.5,
        "ssm_glu_b": 0.02 * nrm(ks[17], (L, SSM_WIDTH), f32),
        "branch_norm_g": gain(ks[18], MIX_WIDTH),
        "w_out": nrm(ks[19], (L, MIX_WIDTH, D), f32) * MIX_WIDTH ** -0.5,
        "post_norm_g": gain(ks[20], D),
        "xa_pre_g": gain(ks[21], D),
        "xa_mem_g": gain(ks[22], D),
        "xa_wq": nrm(ks[23], (L, D, D), f32) * D ** -0.5,
        "xa_wk": nrm(ks[24], (L, D, D), f32) * D ** -0.5,
        "xa_wv": nrm(ks[25], (L, D, D), f32) * D ** -0.5,
        "xa_wo": nrm(ks[26], (L, D, D), f32) * D ** -0.5,
        "xa_post_g": gain(ks[27], D),
    }


def reference(x, mem, pre_norm_g, w_in, conv_w, conv_b, conv_ln_g, conv_ln_b,
              ssm_lambda_re, ssm_lambda_im, ssm_log_dt, ssm_b_re, ssm_b_im, ssm_c_re, ssm_c_im,
              ssm_d, ssm_glu_w, ssm_glu_b, branch_norm_g, w_out, post_norm_g,
              xa_pre_g, xa_mem_g, xa_wq, xa_wk, xa_wv, xa_wo, xa_post_g):
    bsz, seq, _ = x.shape
    offsets = np.cumsum(IN_SIZES)[:-1].tolist()
    gain_offsets = [CONV_WIDTH, CONV_WIDTH + SB_WIDTH]
    for l in range(DEPTH):
        h = rmsnorm(x, pre_norm_g[l])
        proj = h @ w_in[l]
        a_val, a_glu, a_gate, q, k, v, b_gate, c_in, c_gate = jnp.split(proj, offsets, axis=-1)

        y_a = conformer_conv(a_val, a_glu, conv_w[l], conv_b[l], conv_ln_g[l], conv_ln_b[l])
        y_a = y_a * jax.nn.silu(a_gate)

        hs = (bsz, seq, SB_HEADS, SB_HEAD_DIM)
        y_b = stick_breaking_attention(q.reshape(hs), k.reshape(hs), v.reshape(hs))
        y_b = y_b * jax.nn.silu(b_gate)

        y_c = s5_ssm(c_in, ssm_lambda_re[l], ssm_lambda_im[l], ssm_log_dt[l],
                     ssm_b_re[l], ssm_b_im[l], ssm_c_re[l], ssm_c_im[l], ssm_d[l],
                     ssm_glu_w[l], ssm_glu_b[l])
        y_c = y_c * jax.nn.silu(c_gate)

        g_a, g_b, g_c = jnp.split(branch_norm_g[l], gain_offsets)
        y = jnp.concatenate([rmsnorm(y_a, g_a), rmsnorm(y_b, g_b), rmsnorm(y_c, g_c)], axis=-1)
        x = x + rmsnorm(y @ w_out[l], post_norm_g[l])

        h = rmsnorm(x, xa_pre_g[l])
        m = rmsnorm(mem, xa_mem_g[l])
        o = memory_cross_attention(h, m, xa_wq[l], xa_wk[l], xa_wv[l], xa_wo[l])
        x = x + rmsnorm(o, xa_post_g[l])
    return x
```

```python
import functools
import math

import jax
import jax.numpy as jnp
from jax import lax
from jax.experimental import pallas as pl
from jax.experimental.pallas import tpu as pltpu

F32 = jnp.float32
BF16 = jnp.bfloat16

D_MODEL = 4096
CONV_WIDTH = 1024
SB_WIDTH = 2048
SSM_WIDTH = 1024
CONV_TAPS = 31
SB_HEAD_DIM = 128
SB_HEADS = SB_WIDTH // SB_HEAD_DIM
SB_BLOCK = 128
SSM_GROUP = 16
SSM_GROUPS = SSM_WIDTH // SSM_GROUP
SSM_STATE = 64
XA_HEADS = 4
XA_HEAD_DIM = D_MODEL // XA_HEADS
EPS = 1e-6

COL_A_VAL = 0
COL_A_GLU = CONV_WIDTH
COL_A_GATE = 2 * CONV_WIDTH
COL_Q = 3 * CONV_WIDTH
COL_K = COL_Q + SB_WIDTH
COL_V = COL_K + SB_WIDTH
COL_B_GATE = COL_V + SB_WIDTH
COL_C_IN = COL_B_GATE + SB_WIDTH
COL_C_GATE = COL_C_IN + SSM_WIDTH
IN_WIDTH = COL_C_GATE + SSM_WIDTH

LANES = 128
SUBLANES = 8
VMEM_LIMIT_BYTES = 48 * 1024 * 1024

SG_GROUPS = 16
N_SG = SSM_GROUPS // SG_GROUPS
SG_IN = SG_GROUPS * SSM_GROUP
SG_STATE = SG_GROUPS * SSM_STATE
HALO = 32


def _params(*sem):
    return pltpu.CompilerParams(dimension_semantics=sem, vmem_limit_bytes=VMEM_LIMIT_BYTES)


def _rms(x, g):
    return x * lax.rsqrt(jnp.mean(x * x, axis=-1, keepdims=True) + EPS) * g


def _rmsnorm_cast_kernel(x_ref, g_ref, o_ref):
    o_ref[...] = _rms(x_ref[...], g_ref[...]).astype(o_ref.dtype)


def rmsnorm_cast(x, g, *, tm=256):
    m, d = x.shape
    return pl.pallas_call(
        _rmsnorm_cast_kernel,
        grid=(m // tm,),
        in_specs=[pl.BlockSpec((tm, d), lambda i: (i, 0)),
                  pl.BlockSpec((1, d), lambda i: (0, 0))],
        out_specs=pl.BlockSpec((tm, d), lambda i: (i, 0)),
        out_shape=jax.ShapeDtypeStruct((m, d), BF16),
        compiler_params=_params("parallel"),
        name="rmsnorm_cast",
    )(x, g.reshape(1, d))


def _residual_norm_kernel(o_ref, x_ref, gp_ref, gn_ref, xo_ref, ho_ref):
    x1 = x_ref[...] + _rms(o_ref[...].astype(F32), gp_ref[...])
    xo_ref[...] = x1
    ho_ref[...] = _rms(x1, gn_ref[...]).astype(ho_ref.dtype)


def _residual_kernel(o_ref, x_ref, gp_ref, xo_ref):
    xo_ref[...] = x_ref[...] + _rms(o_ref[...].astype(F32), gp_ref[...])


def residual_norm(o, x, g_post, g_next, *, tm=256):
    m, d = x.shape
    row = pl.BlockSpec((tm, d), lambda i: (i, 0))
    vec = pl.BlockSpec((1, d), lambda i: (0, 0))
    if g_next is None:
        return pl.pallas_call(
            _residual_kernel, grid=(m // tm,),
            in_specs=[row, row, vec], out_specs=row,
            out_shape=jax.ShapeDtypeStruct((m, d), F32),
            compiler_params=_params("parallel"), name="residual",
        )(o, x, g_post.reshape(1, d)), None
    return pl.pallas_call(
        _residual_norm_kernel, grid=(m // tm,),
        in_specs=[row, row, vec, vec], out_specs=[row, row],
        out_shape=[jax.ShapeDtypeStruct((m, d), F32), jax.ShapeDtypeStruct((m, d), BF16)],
        compiler_params=_params("parallel"), name="residual_norm",
    )(o, x, g_post.reshape(1, d), g_next.reshape(1, d))


def _matmul_kernel(a_ref, w_ref, o_ref):
    o_ref[...] = jnp.dot(a_ref[...], w_ref[...], preferred_element_type=F32).astype(o_ref.dtype)


def matmul(a, w, *, tm, tn, out_dtype=BF16, name="matmul"):
    m, k = a.shape
    _, n = w.shape
    tm = min(tm, m)
    return pl.pallas_call(
        _matmul_kernel,
        grid=(m // tm, n // tn),
        in_specs=[pl.BlockSpec((tm, k), lambda i, j: (i, 0)),
                  pl.BlockSpec((k, tn), lambda i, j: (0, j))],
        out_specs=pl.BlockSpec((tm, tn), lambda i, j: (i, j)),
        out_shape=jax.ShapeDtypeStruct((m, n), out_dtype),
        compiler_params=_params("parallel", "arbitrary"),
        name=name,
    )(a, w)


def _conv_kernel(val_ref, glu_ref, w_ref, b_ref, g_ref, beta_ref, o_ref, ubuf, cbuf, *, t_rows):
    @pl.when(pl.program_id(1) == 0)
    def _():
        ubuf[0:HALO, :] = jnp.zeros((HALO, CONV_WIDTH), F32)

    u = val_ref[...].astype(F32) * jax.nn.sigmoid(glu_ref[...].astype(F32))
    ubuf[HALO:HALO + t_rows, :] = u

    first = HALO - (CONV_TAPS - 1)

    def lane_chunk(c, carry):
        cs = pl.ds(pl.multiple_of(c * LANES, LANES), LANES)
        acc = jnp.zeros((t_rows, LANES), F32)
        for k in range(CONV_TAPS):
            acc = acc + w_ref[pl.ds(k, 1), cs] * ubuf[pl.ds(first + k, t_rows), cs]
        cbuf[:, cs] = acc
        return carry

    lax.fori_loop(0, CONV_WIDTH // LANES, lane_chunk, 0)
    ubuf[0:HALO, :] = ubuf[t_rows:t_rows + HALO, :]

    y = cbuf[...] + b_ref[...]
    mu = jnp.mean(y, axis=-1, keepdims=True)
    yc = y - mu
    yn = yc * lax.rsqrt(jnp.mean(yc * yc, axis=-1, keepdims=True) + EPS) * g_ref[...] + beta_ref[...]
    o_ref[...] = (yn * jax.nn.sigmoid(yn)).astype(o_ref.dtype)


def conformer_conv(proj, conv_w, conv_b, ln_g, ln_b, *, batch, seq, t_rows=128):
    m = batch * seq
    nt = seq // t_rows
    w = jnp.zeros((HALO, CONV_WIDTH), F32).at[:CONV_TAPS].set(conv_w)
    vec = pl.BlockSpec((1, CONV_WIDTH), lambda b, t: (0, 0))
    return pl.pallas_call(
        functools.partial(_conv_kernel, t_rows=t_rows),
        grid=(batch, nt),
        in_specs=[pl.BlockSpec((t_rows, CONV_WIDTH), lambda b, t: (b * nt + t, COL_A_VAL // CONV_WIDTH)),
                  pl.BlockSpec((t_rows, CONV_WIDTH), lambda b, t: (b * nt + t, COL_A_GLU // CONV_WIDTH)),
                  pl.BlockSpec((HALO, CONV_WIDTH), lambda b, t: (0, 0)),
                  vec, vec, vec],
        out_specs=pl.BlockSpec((t_rows, CONV_WIDTH), lambda b, t: (b * nt + t, 0)),
        out_shape=jax.ShapeDtypeStruct((m, CONV_WIDTH), BF16),
        scratch_shapes=[pltpu.VMEM((t_rows + HALO, CONV_WIDTH), F32),
                        pltpu.VMEM((t_rows, CONV_WIDTH), F32)],
        compiler_params=_params("parallel", "arbitrary"),
        name="conformer_conv",
    )(proj, proj, w, conv_b.reshape(1, -1), ln_g.reshape(1, -1), ln_b.reshape(1, -1))


def _sb_kernel(q_ref, k_ref, v_ref, o_ref, *, n_blocks):
    blk = SB_BLOCK
    scale = 1.0 / math.sqrt(SB_HEAD_DIM)
    row = lax.broadcasted_iota(jnp.int32, (blk, blk), 0)
    col = lax.broadcasted_iota(jnp.int32, (blk, blk), 1)
    suffix = jnp.where(row > col, 1.0, 0.0).astype(BF16)
    rhs = jnp.concatenate([suffix, jnp.ones((blk, blk), BF16)], axis=1)
    before = col < row

    def block(qb, j, c, acc, diagonal):
        ks = pl.ds(pl.multiple_of(j * blk, blk), blk)
        z = lax.dot_general(qb, k_ref[ks, :], (((1,), (1,)), ((), ())),
                            preferred_element_type=F32) * scale
        softplus = jnp.maximum(z, 0.0) + jnp.log(1.0 + jnp.exp(-jnp.abs(z)))
        lk = -softplus
        if diagonal:
            lk = jnp.where(before, lk, 0.0)
        hi = lk.astype(BF16)
        lo = (lk - hi.astype(F32)).astype(BF16)
        sums = (jnp.dot(hi, rhs, preferred_element_type=F32)
                + jnp.dot(lo, rhs, preferred_element_type=F32))
        later = c + sums[:, :blk]
        w = jnp.exp(z - softplus + later)
        if diagonal:
            w = jnp.where(before, w, 0.0)
        acc = acc + jnp.dot(w.astype(BF16), v_ref[ks, :], preferred_element_type=F32)
        return c + sums[:, blk:], acc

    def q_block(i, carry):
        qs = pl.ds(pl.multiple_of(i * blk, blk), blk)
        qb = q_ref[qs, :]
        zero = jnp.zeros((blk, blk), F32)
        c, acc = block(qb, i, zero, zero, True)

        def kv_block(jj, ca):
            return block(qb, i - 1 - jj, ca[0], ca[1], False)

        c, acc = lax.fori_loop(0, i, kv_block, (c, acc))
        o_ref[qs, :] = acc.astype(o_ref.dtype)
        return carry

    lax.fori_loop(0, n_blocks, q_block, 0)


def stick_breaking_attention(proj, *, batch, seq):
    m = batch * seq
    qc, kc, vc = (c // SB_HEAD_DIM for c in (COL_Q, COL_K, COL_V))
    return pl.pallas_call(
        functools.partial(_sb_kernel, n_blocks=seq // SB_BLOCK),
        grid=(batch, SB_HEADS),
        in_specs=[pl.BlockSpec((seq, SB_HEAD_DIM), lambda b, h: (b, qc + h)),
                  pl.BlockSpec((seq, SB_HEAD_DIM), lambda b, h: (b, kc + h)),
                  pl.BlockSpec((seq, SB_HEAD_DIM), lambda b, h: (b, vc + h))],
        out_specs=pl.BlockSpec((seq, SB_HEAD_DIM), lambda b, h: (b, h)),
        out_shape=jax.ShapeDtypeStruct((m, SB_WIDTH), BF16),
        compiler_params=_params("parallel", "parallel"),
        name="stick_breaking",
    )(proj, proj, proj)


def _cmul(ar, ai, br, bi):
    return ar * br - ai * bi, ar * bi + ai * br


def _ssm_param_kernel(lr_ref, li_ref, ldt_ref, lrr_ref, lir_ref, br_ref, bi_ref,
                      pr_ref, pi_ref, bbr_ref, bbi_ref):
    def zoh(lr, li, dt):
        mag = jnp.exp(lr * dt)
        ang = li * dt
        er, ei = mag * jnp.cos(ang), mag * jnp.sin(ang)
        nr, ni = er - 1.0, ei
        den = lr * lr + li * li
        return er, ei, (nr * lr + ni * li) / den, (ni * lr - nr * li) / den

    dt = jnp.exp(ldt_ref[...])
    er, ei, _, _ = zoh(lr_ref[...], li_ref[...], dt)
    pr, pi = er, ei
    pr_ref[0], pi_ref[0] = pr, pi
    for n in range(1, SUBLANES):
        pr, pi = _cmul(pr, pi, er, ei)
        pr_ref[n], pi_ref[n] = pr, pi

    _, _, cr, ci = zoh(lrr_ref[...], lir_ref[...], dt)
    bbr_ref[...], bbi_ref[...] = _cmul(cr, ci, br_ref[...], bi_ref[...])


def ssm_params(lam_re, lam_im, log_dt, b_re, b_im, c_re, c_im):
    g, p, c = SSM_GROUPS, SSM_STATE, SSM_GROUP
    full = lambda shape: pl.BlockSpec(shape, lambda: (0,) * len(shape))
    pw_re, pw_im, bb_re, bb_im = pl.pallas_call(
        _ssm_param_kernel,
        in_specs=[full((g, p)), full((g, p)), full((g, 1)), full((g, p * c)), full((g, p * c)),
                  full((g, p * c)), full((g, p * c))],
        out_specs=[full((SUBLANES, g, p)), full((SUBLANES, g, p)), full((g, p * c)), full((g, p * c))],
        out_shape=[jax.ShapeDtypeStruct((SUBLANES, g, p), F32)] * 2
        + [jax.ShapeDtypeStruct((g, p * c), F32)] * 2,
        name="ssm_params",
    )(lam_re, lam_im, log_dt.reshape(g, 1), jnp.repeat(lam_re, c, axis=1), jnp.repeat(lam_im, c, axis=1),
      b_re.reshape(g, p * c), b_im.reshape(g, p * c))

    eye = jnp.eye(SG_GROUPS, dtype=F32)

    def in_map(bb):
        t = bb.reshape(N_SG, SG_GROUPS, p, c).transpose(0, 1, 3, 2)
        return jnp.einsum("ab,sacp->sacbp", eye, t).reshape(N_SG, SG_IN, SG_STATE)

    def out_map(cc):
        t = cc.reshape(N_SG, SG_GROUPS, c, p).transpose(0, 1, 3, 2)
        return jnp.einsum("ab,sapc->sapbc", eye, t).reshape(N_SG, SG_STATE, SG_IN)

    b_map = jnp.concatenate([in_map(bb_re), in_map(bb_im)], axis=2).astype(BF16)
    c_map = jnp.concatenate([out_map(c_re), out_map(-c_im)], axis=1).astype(BF16)

    pw = jnp.stack([pw_re, pw_im], axis=1).reshape(SUBLANES, 2, N_SG, SG_STATE)
    r = jnp.arange(SUBLANES)
    kinds = [jnp.where((r >= s)[None, None, :, None], pw[s - 1].transpose(1, 0, 2)[:, :, None, :], 0.0)
             for s in (1, 2, 4)]
    kinds.append(pw.transpose(2, 1, 0, 3))
    tables = jnp.stack(kinds, axis=1)
    return b_map, c_map, tables


def _ssm_kernel(u_ref, bmap_ref, cmap_ref, tab_ref, d_ref, gw_ref, gb_ref, o_ref,
                hbuf, ybuf, carry_ref, *, t_rows):
    @pl.when(pl.program_id(1) == 0)
    def _():
        carry_ref[...] = jnp.zeros(carry_ref.shape, F32)

    def fma(xr, xi, ar, ai, sr, si):
        return xr + ar * sr - ai * si, xi + ar * si + ai * sr

    for sg in range(N_SG):
        u_sg = u_ref[:, sg * SG_IN:(sg + 1) * SG_IN]
        hbuf[...] = jnp.dot(u_sg, bmap_ref[sg], preferred_element_type=F32)

        def row_block(i, carry, sg=sg):
            cr, ci = carry
            rs = pl.ds(pl.multiple_of(i * SUBLANES, SUBLANES), SUBLANES)
            xr = hbuf[rs, 0:SG_STATE]
            xi = hbuf[rs, SG_STATE:2 * SG_STATE]
            for kind, shift in enumerate((1, 2, 4)):
                sr = pltpu.roll(xr, shift, axis=0)
                si = pltpu.roll(xi, shift, axis=0)
                xr, xi = fma(xr, xi, tab_ref[sg, kind, 0], tab_ref[sg, kind, 1], sr, si)
            xr, xi = fma(xr, xi, tab_ref[sg, 3, 0], tab_ref[sg, 3, 1], cr, ci)
            hbuf[rs, 0:SG_STATE] = xr
            hbuf[rs, SG_STATE:2 * SG_STATE] = xi
            last = SUBLANES - 1
            return (jnp.broadcast_to(xr[last:last + 1, :], xr.shape),
                    jnp.broadcast_to(xi[last:last + 1, :], xi.shape))

        cr, ci = lax.fori_loop(0, t_rows // SUBLANES, row_block, (carry_ref[sg, 0], carry_ref[sg, 1]))
        carry_ref[sg, 0] = cr
        carry_ref[sg, 1] = ci
        ybuf[:, sg * SG_IN:(sg + 1) * SG_IN] = jnp.dot(hbuf[...].astype(BF16), cmap_ref[sg],
                                                       preferred_element_type=F32)

    y = ybuf[...] + d_ref[...] * u_ref[...].astype(F32)
    y = 0.5 * y * (1.0 + jnp.tanh(math.sqrt(2.0 / math.pi) * (y + 0.044715 * (y * y * y))))
    gate = jnp.dot(y.astype(BF16), gw_ref[...], preferred_element_type=F32) + gb_ref[...]
    o_ref[...] = (y * jax.nn.sigmoid(gate)).astype(o_ref.dtype)


def s5_ssm(proj, b_map, c_map, tables, d_skip, glu_w, glu_b, *, batch, seq, t_rows=256):
    m = batch * seq
    nt = seq // t_rows
    const = lambda shape: pl.BlockSpec(shape, lambda b, t: (0,) * len(shape))
    return pl.pallas_call(
        functools.partial(_ssm_kernel, t_rows=t_rows),
        grid=(batch, nt),
        in_specs=[pl.BlockSpec((t_rows, SSM_WIDTH), lambda b, t: (b * nt + t, COL_C_IN // SSM_WIDTH)),
                  const(b_map.shape), const(c_map.shape), const(tables.shape),
                  const((1, SSM_WIDTH)), const((SSM_WIDTH, SSM_WIDTH)), const((1, SSM_WIDTH))],
        out_specs=pl.BlockSpec((t_rows, SSM_WIDTH), lambda b, t: (b * nt + t, 0)),
        out_shape=jax.ShapeDtypeStruct((m, SSM_WIDTH), BF16),
        scratch_shapes=[pltpu.VMEM((t_rows, 2 * SG_STATE), F32),
                        pltpu.VMEM((t_rows, SSM_WIDTH), F32),
                        pltpu.VMEM((N_SG, 2, SUBLANES, SG_STATE), F32)],
        compiler_params=_params("parallel", "arbitrary"),
        name="s5_ssm",
    )(proj, b_map, c_map, tables, d_skip.reshape(1, -1), glu_w, glu_b.reshape(1, -1))


def _silu(x):
    return x * jax.nn.sigmoid(x)


def _branch_kernel(ya_ref, ga_ref, yb_ref, gb0_ref, gb1_ref, yc_ref, gc_ref, g_ref, o_ref):
    a0, b0, c0 = 0, CONV_WIDTH, CONV_WIDTH + SB_WIDTH
    ya = ya_ref[...].astype(F32) * _silu(ga_ref[...].astype(F32))
    o_ref[:, a0:b0] = _rms(ya, g_ref[:, a0:b0]).astype(o_ref.dtype)
    gate_b = jnp.concatenate([gb0_ref[...], gb1_ref[...]], axis=1).astype(F32)
    yb = yb_ref[...].astype(F32) * _silu(gate_b)
    o_ref[:, b0:c0] = _rms(yb, g_ref[:, b0:c0]).astype(o_ref.dtype)
    yc = yc_ref[...].astype(F32) * _silu(gc_ref[...].astype(F32))
    o_ref[:, c0:] = _rms(yc, g_ref[:, c0:]).astype(o_ref.dtype)


def branch_gate_norm(ya, yb, yc, proj, g, *, tm=256):
    m = ya.shape[0]
    w = CONV_WIDTH
    pcol = lambda c: pl.BlockSpec((tm, w), lambda i: (i, c // w))
    return pl.pallas_call(
        _branch_kernel,
        grid=(m // tm,),
        in_specs=[pl.BlockSpec((tm, CONV_WIDTH), lambda i: (i, 0)), pcol(COL_A_GATE),
                  pl.BlockSpec((tm, SB_WIDTH), lambda i: (i, 0)), pcol(COL_B_GATE), pcol(COL_B_GATE + w),
                  pl.BlockSpec((tm, SSM_WIDTH), lambda i: (i, 0)), pcol(COL_C_GATE),
                  pl.BlockSpec((1, D_MODEL), lambda i: (0, 0))],
        out_specs=pl.BlockSpec((tm, D_MODEL), lambda i: (i, 0)),
        out_shape=jax.ShapeDtypeStruct((m, D_MODEL), BF16),
        compiler_params=_params("parallel"),
        name="branch_gate_norm",
    )(ya, proj, yb, proj, proj, yc, proj, g.reshape(1, -1))


def _xattn_kernel(q_ref, k_ref, v_ref, o_ref):
    s = lax.dot_general(q_ref[...], k_ref[...], (((1,), (1,)), ((), ())),
                        preferred_element_type=F32) / math.sqrt(XA_HEAD_DIM)
    e = jnp.exp(s - jnp.max(s, axis=-1, keepdims=True))
    p = e / jnp.sum(e, axis=-1, keepdims=True)
    o_ref[...] = jnp.dot(p.astype(BF16), v_ref[...], preferred_element_type=F32).astype(o_ref.dtype)


def memory_attention(q, k, v, *, batch, seq, mem_len, tq=512):
    m = batch * seq
    nq = seq // tq
    return pl.pallas_call(
        _xattn_kernel,
        grid=(batch, XA_HEADS, nq),
        in_specs=[pl.BlockSpec((tq, XA_HEAD_DIM), lambda b, h, i: (b * nq + i, h)),
                  pl.BlockSpec((mem_len, XA_HEAD_DIM), lambda b, h, i: (b, h)),
                  pl.BlockSpec((mem_len, XA_HEAD_DIM), lambda b, h, i: (b, h))],
        out_specs=pl.BlockSpec((tq, XA_HEAD_DIM), lambda b, h, i: (b * nq + i, h)),
        out_shape=jax.ShapeDtypeStruct((m, D_MODEL), BF16),
        compiler_params=_params("parallel", "parallel", "arbitrary"),
        name="memory_attention",
    )(q, k, v)


def kernel(x, mem, pre_norm_g, w_in, conv_w, conv_b, conv_ln_g, conv_ln_b, ssm_lambda_re, ssm_lambda_im, ssm_log_dt, ssm_b_re, ssm_b_im, ssm_c_re, ssm_c_im, ssm_d, ssm_glu_w, ssm_glu_b, branch_norm_g, w_out, post_norm_g, xa_pre_g, xa_mem_g, xa_wq, xa_wk, xa_wv, xa_wo, xa_post_g):
    batch, seq, d = x.shape
    mem_len = mem.shape[1]
    depth = w_in.shape[0]
    xs = x.reshape(batch * seq, d)
    mems = mem.reshape(batch * mem_len, d)

    h = rmsnorm_cast(xs, pre_norm_g[0])
    for l in range(depth):
        proj = matmul(h, w_in[l].astype(BF16), tm=1024, tn=512, name="in_proj")
        ya = conformer_conv(proj, conv_w[l], conv_b[l], conv_ln_g[l], conv_ln_b[l], batch=batch, seq=seq)
        yb = stick_breaking_attention(proj, batch=batch, seq=seq)
        b_map, c_map, tables = ssm_params(ssm_lambda_re[l], ssm_lambda_im[l], ssm_log_dt[l],
                                          ssm_b_re[l], ssm_b_im[l], ssm_c_re[l], ssm_c_im[l])
        yc = s5_ssm(proj, b_map, c_map, tables, ssm_d[l], ssm_glu_w[l].astype(BF16), ssm_glu_b[l],
                    batch=batch, seq=seq)
        y = branch_gate_norm(ya, yb, yc, proj, branch_norm_g[l])
        o = matmul(y, w_out[l].astype(BF16), tm=1024, tn=512, name="out_proj")
        xs, h2 = residual_norm(o, xs, post_norm_g[l], xa_pre_g[l])

        mn = rmsnorm_cast(mems, xa_mem_g[l])
        q = matmul(h2, xa_wq[l].astype(BF16), tm=1024, tn=512, name="xa_q")
        k = matmul(mn, xa_wk[l].astype(BF16), tm=1024, tn=512, name="xa_k")
        v = matmul(mn, xa_wv[l].astype(BF16), tm=1024, tn=512, name="xa_v")
        a = memory_attention(q, k, v, batch=batch, seq=seq, mem_len=mem_len)
        o = matmul(a, xa_wo[l].astype(BF16), tm=1024, tn=512, name="xa_o")
        g_next = pre_norm_g[l + 1] if l + 1 < depth else None
        xs, h = residual_norm(o, xs, xa_post_g[l], g_next)
    return xs.reshape(batch, seq, d)
```

```python
import functools
import math

import jax
import jax.numpy as jnp
from jax import lax
from jax.experimental import pallas as pl
from jax.experimental.pallas import tpu as pltpu

F32 = jnp.float32
BF16 = jnp.bfloat16

D_MODEL = 4096
CONV_WIDTH = 1024
SB_WIDTH = 2048
SSM_WIDTH = 1024
CONV_TAPS = 31
SB_HEAD_DIM = 128
SB_HEADS = SB_WIDTH // SB_HEAD_DIM
SB_BLOCK = 128
SSM_GROUP = 16
SSM_GROUPS = SSM_WIDTH // SSM_GROUP
SSM_STATE = 64
XA_HEADS = 4
XA_HEAD_DIM = D_MODEL // XA_HEADS
EPS = 1e-6

COL_A_VAL = 0
COL_A_GLU = CONV_WIDTH
COL_A_GATE = 2 * CONV_WIDTH
COL_Q = 3 * CONV_WIDTH
COL_K = COL_Q + SB_WIDTH
COL_V = COL_K + SB_WIDTH
COL_B_GATE = COL_V + SB_WIDTH
COL_C_IN = COL_B_GATE + SB_WIDTH
COL_C_GATE = COL_C_IN + SSM_WIDTH
IN_WIDTH = COL_C_GATE + SSM_WIDTH

LANES = 128
SUBLANES = 8
VMEM_LIMIT_BYTES = 48 * 1024 * 1024

SG_GROUPS = 16
N_SG = SSM_GROUPS // SG_GROUPS
SG_IN = SG_GROUPS * SSM_GROUP
SG_STATE = SG_GROUPS * SSM_STATE
HALO = 32


def _params(*sem):
    return pltpu.CompilerParams(dimension_semantics=sem, vmem_limit_bytes=VMEM_LIMIT_BYTES)


def _rms(x, g):
    return x * lax.rsqrt(jnp.mean(x * x, axis=-1, keepdims=True) + EPS) * g


def _rmsnorm_cast_kernel(x_ref, g_ref, o_ref):
    o_ref[...] = _rms(x_ref[...], g_ref[...]).astype(o_ref.dtype)


def rmsnorm_cast(x, g, *, tm=256):
    m, d = x.shape
    return pl.pallas_call(
        _rmsnorm_cast_kernel,
        grid=(m // tm,),
        in_specs=[pl.BlockSpec((tm, d), lambda i: (i, 0)),
                  pl.BlockSpec((1, d), lambda i: (0, 0))],
        out_specs=pl.BlockSpec((tm, d), lambda i: (i, 0)),
        out_shape=jax.ShapeDtypeStruct((m, d), BF16),
        compiler_params=_params("parallel"),
        name="rmsnorm_cast",
    )(x, g.reshape(1, d))


def _residual_norm_kernel(o_ref, x_ref, gp_ref, gn_ref, xo_ref, ho_ref):
    x1 = x_ref[...] + _rms(o_ref[...].astype(F32), gp_ref[...])
    xo_ref[...] = x1
    ho_ref[...] = _rms(x1, gn_ref[...]).astype(ho_ref.dtype)


def _residual_kernel(o_ref, x_ref, gp_ref, xo_ref):
    xo_ref[...] = x_ref[...] + _rms(o_ref[...].astype(F32), gp_ref[...])


def residual_norm(o, x, g_post, g_next, *, tm=256):
    m, d = x.shape
    row = pl.BlockSpec((tm, d), lambda i: (i, 0))
    vec = pl.BlockSpec((1, d), lambda i: (0, 0))
    if g_next is None:
        return pl.pallas_call(
            _residual_kernel, grid=(m // tm,),
            in_specs=[row, row, vec], out_specs=row,
            out_shape=jax.ShapeDtypeStruct((m, d), F32),
            compiler_params=_params("parallel"), name="residual",
        )(o, x, g_post.reshape(1, d)), None
    return pl.pallas_call(
        _residual_norm_kernel, grid=(m // tm,),
        in_specs=[row, row, vec, vec], out_specs=[row, row],
        out_shape=[jax.ShapeDtypeStruct((m, d), F32), jax.ShapeDtypeStruct((m, d), BF16)],
        compiler_params=_params("parallel"), name="residual_norm",
    )(o, x, g_post.reshape(1, d), g_next.reshape(1, d))


def _matmul_kernel(a_ref, w_ref, o_ref, wb_ref):
    @pl.when(pl.program_id(1) == 0)
    def _():
        wb_ref[...] = w_ref[...].astype(BF16)

    o_ref[...] = jnp.dot(a_ref[...], wb_ref[...], preferred_element_type=F32).astype(o_ref.dtype)


def matmul(a, w, layer, *, tm, tn, out_dtype=BF16, name="matmul"):
    m, k = a.shape
    _, _, n = w.shape
    tm = min(tm, m)
    return pl.pallas_call(
        _matmul_kernel,
        grid=(n // tn, m // tm),
        in_specs=[pl.BlockSpec((tm, k), lambda j, i: (i, 0)),
                  pl.BlockSpec((None, k, tn), lambda j, i: (layer, 0, j))],
        out_specs=pl.BlockSpec((tm, tn), lambda j, i: (i, j)),
        out_shape=jax.ShapeDtypeStruct((m, n), out_dtype),
        scratch_shapes=[pltpu.VMEM((k, tn), BF16)],
        compiler_params=_params("parallel", "arbitrary"),
        name=name,
    )(a, w)


def _conv_kernel(val_ref, glu_ref, w_ref, b_ref, g_ref, beta_ref, o_ref, ubuf, sbuf, cbuf, *, t_rows):
    @pl.when(pl.program_id(1) == 0)
    def _():
        ubuf[0:HALO, :] = jnp.zeros((HALO, CONV_WIDTH), F32)

    u = val_ref[...].astype(F32) * jax.nn.sigmoid(glu_ref[...].astype(F32))
    ubuf[HALO:HALO + t_rows, :] = u

    first = HALO - (CONV_TAPS - 1)
    span = t_rows + HALO - SUBLANES

    def lane_chunk(c, carry):
        cs = pl.ds(pl.multiple_of(c * LANES, LANES), LANES)
        for r in range(1, SUBLANES):
            sbuf[r, 0:span, :] = ubuf[pl.ds(r, span), cs]
        acc = jnp.zeros((t_rows, LANES), F32)
        for k in range(CONV_TAPS):
            a, r = divmod(first + k, SUBLANES)
            rows = pl.ds(a * SUBLANES, t_rows)
            shifted = ubuf[rows, cs] if r == 0 else sbuf[r, rows, :]
            acc = acc + w_ref[pl.ds(k, 1), cs] * shifted
        cbuf[:, cs] = acc
        return carry

    lax.fori_loop(0, CONV_WIDTH // LANES, lane_chunk, 0)
    ubuf[0:HALO, :] = ubuf[t_rows:t_rows + HALO, :]

    y = cbuf[...] + b_ref[...]
    mu = jnp.mean(y, axis=-1, keepdims=True)
    yc = y - mu
    yn = yc * lax.rsqrt(jnp.mean(yc * yc, axis=-1, keepdims=True) + EPS) * g_ref[...] + beta_ref[...]
    o_ref[...] = (yn * jax.nn.sigmoid(yn)).astype(o_ref.dtype)


def conformer_conv(proj, conv_w, conv_b, ln_g, ln_b, *, batch, seq, t_rows=128):
    m = batch * seq
    nt = seq // t_rows
    w = jnp.zeros((HALO, CONV_WIDTH), F32).at[:CONV_TAPS].set(conv_w)
    vec = pl.BlockSpec((1, CONV_WIDTH), lambda b, t: (0, 0))
    return pl.pallas_call(
        functools.partial(_conv_kernel, t_rows=t_rows),
        grid=(batch, nt),
        in_specs=[pl.BlockSpec((t_rows, CONV_WIDTH), lambda b, t: (b * nt + t, COL_A_VAL // CONV_WIDTH)),
                  pl.BlockSpec((t_rows, CONV_WIDTH), lambda b, t: (b * nt + t, COL_A_GLU // CONV_WIDTH)),
                  pl.BlockSpec((HALO, CONV_WIDTH), lambda b, t: (0, 0)),
                  vec, vec, vec],
        out_specs=pl.BlockSpec((t_rows, CONV_WIDTH), lambda b, t: (b * nt + t, 0)),
        out_shape=jax.ShapeDtypeStruct((m, CONV_WIDTH), BF16),
        scratch_shapes=[pltpu.VMEM((t_rows + HALO, CONV_WIDTH), F32),
                        pltpu.VMEM((SUBLANES, t_rows + HALO - SUBLANES, LANES), F32),
                        pltpu.VMEM((t_rows, CONV_WIDTH), F32)],
        compiler_params=_params("parallel", "arbitrary"),
        name="conformer_conv",
    )(proj, proj, w, conv_b.reshape(1, -1), ln_g.reshape(1, -1), ln_b.reshape(1, -1))


SB_TILE = 256


def _sb_kernel(q_ref, k_ref, v_ref, o_ref, q2_ref, c_ref, acc_ref, *, n_tiles, n_heads):
    t = SB_TILE
    log2e = 1.0 / math.log(2.0)
    row = lax.broadcasted_iota(jnp.int32, (t, t), 0)
    col = lax.broadcasted_iota(jnp.int32, (t, t), 1)
    suffix = jnp.where(row > col, 1.0, 0.0).astype(BF16)
    before = col < row

    heads = range(n_heads)
    lanes = [slice(hh * SB_HEAD_DIM, (hh + 1) * SB_HEAD_DIM) for hh in heads]

    def tiles(j, diagonal):
        ks = pl.ds(pl.multiple_of(j * t, t), t)
        zs = [lax.dot_general(q2_ref[hh], k_ref[ks, lanes[hh]], (((1,), (1,)), ((), ())),
                              preferred_element_type=F32) for hh in heads]
        sps = [jnp.maximum(z, 0.0) + jnp.log(1.0 + jnp.exp2(-jnp.abs(z))) * log2e for z in zs]
        drop = [jnp.where(before, sp, 0.0) for sp in sps] if diagonal else sps
        inner = [jnp.dot(d.astype(BF16), suffix, preferred_element_type=F32) for d in drop]
        ws = []
        for hh in heads:
            later = inner[hh] if diagonal else inner[hh] + jnp.concatenate([c_ref[hh]] * (t // LANES), axis=1)
            w = jnp.exp2(zs[hh] - sps[hh] - later)
            ws.append(jnp.where(before, w, 0.0) if diagonal else w)
        pvs = [jnp.dot(w.astype(BF16), v_ref[ks, hs], preferred_element_type=F32)
               for w, hs in zip(ws, lanes)]
        for hh in heads:
            total = jnp.broadcast_to(jnp.sum(drop[hh], axis=-1, keepdims=True), (t, LANES))
            c_ref[hh] = total if diagonal else c_ref[hh] + total
            acc_ref[hh] = pvs[hh] if diagonal else acc_ref[hh] + pvs[hh]

    def q_tile(i, carry):
        qs = pl.ds(pl.multiple_of(i * t, t), t)
        for hh in heads:
            q2 = q_ref[qs, lanes[hh]].astype(F32) * (log2e / math.sqrt(SB_HEAD_DIM))
            q2_ref[hh] = q2.astype(BF16)
        tiles(i, True)

        def kv_tile(jj, ca):
            tiles(i - 1 - jj, False)
            return ca

        lax.fori_loop(0, i, kv_tile, 0)
        for hh in heads:
            o_ref[qs, lanes[hh]] = acc_ref[hh].astype(o_ref.dtype)
        return carry

    lax.fori_loop(0, n_tiles, q_tile, 0)


def stick_breaking_attention(proj, *, batch, seq, heads_per_step=8):
    m = batch * seq
    width = heads_per_step * SB_HEAD_DIM
    qc, kc, vc = (c // width for c in (COL_Q, COL_K, COL_V))
    blockspec = lambda c0: pl.BlockSpec((seq, width), lambda b, h: (b, c0 + h))
    return pl.pallas_call(
        functools.partial(_sb_kernel, n_tiles=seq // SB_TILE, n_heads=heads_per_step),
        grid=(batch, SB_HEADS // heads_per_step),
        in_specs=[blockspec(qc), blockspec(kc), blockspec(vc)],
        out_specs=blockspec(0),
        out_shape=jax.ShapeDtypeStruct((m, SB_WIDTH), BF16),
        scratch_shapes=[pltpu.VMEM((heads_per_step, SB_TILE, SB_HEAD_DIM), BF16),
                        pltpu.VMEM((heads_per_step, SB_TILE, LANES), F32),
                        pltpu.VMEM((heads_per_step, SB_TILE, SB_HEAD_DIM), F32)],
        compiler_params=_params("parallel", "parallel"),
        name="stick_breaking",
    )(proj, proj, proj)


def _cmul(ar, ai, br, bi):
    return ar * br - ai * bi, ar * bi + ai * br


def _ssm_param_kernel(lr_ref, li_ref, ldt_ref, lrr_ref, lir_ref, br_ref, bi_ref,
                      pr_ref, pi_ref, bbr_ref, bbi_ref):
    def zoh(lr, li, dt):
        mag = jnp.exp(lr * dt)
        ang = li * dt
        er, ei = mag * jnp.cos(ang), mag * jnp.sin(ang)
        nr, ni = er - 1.0, ei
        den = lr * lr + li * li
        return er, ei, (nr * lr + ni * li) / den, (ni * lr - nr * li) / den

    dt = jnp.exp(ldt_ref[...])
    er, ei, _, _ = zoh(lr_ref[...], li_ref[...], dt)
    pr, pi = er, ei
    pr_ref[0], pi_ref[0] = pr, pi
    for n in range(1, SUBLANES):
        pr, pi = _cmul(pr, pi, er, ei)
        pr_ref[n], pi_ref[n] = pr, pi

    _, _, cr, ci = zoh(lrr_ref[...], lir_ref[...], dt)
    bbr_ref[...], bbi_ref[...] = _cmul(cr, ci, br_ref[...], bi_ref[...])


def ssm_params(lam_re, lam_im, log_dt, b_re, b_im, c_re, c_im):
    g, p, c = SSM_GROUPS, SSM_STATE, SSM_GROUP
    full = lambda shape: pl.BlockSpec(shape, lambda: (0,) * len(shape))
    pw_re, pw_im, bb_re, bb_im = pl.pallas_call(
        _ssm_param_kernel,
        in_specs=[full((g, p)), full((g, p)), full((g, 1)), full((g, p * c)), full((g, p * c)),
                  full((g, p * c)), full((g, p * c))],
        out_specs=[full((SUBLANES, g, p)), full((SUBLANES, g, p)), full((g, p * c)), full((g, p * c))],
        out_shape=[jax.ShapeDtypeStruct((SUBLANES, g, p), F32)] * 2
        + [jax.ShapeDtypeStruct((g, p * c), F32)] * 2,
        name="ssm_params",
    )(lam_re, lam_im, log_dt.reshape(g, 1), jnp.repeat(lam_re, c, axis=1), jnp.repeat(lam_im, c, axis=1),
      b_re.reshape(g, p * c), b_im.reshape(g, p * c))

    eye = jnp.eye(SG_GROUPS, dtype=F32)

    def in_map(bb):
        t = bb.reshape(N_SG, SG_GROUPS, p, c).transpose(0, 1, 3, 2)
        return jnp.einsum("ab,sacp->sacbp", eye, t).reshape(N_SG, SG_IN, SG_STATE)

    def out_map(cc):
        t = cc.reshape(N_SG, SG_GROUPS, c, p).transpose(0, 1, 3, 2)
        return jnp.einsum("ab,sapc->sapbc", eye, t).reshape(N_SG, SG_STATE, SG_IN)

    b_map = jnp.concatenate([in_map(bb_re), in_map(bb_im)], axis=2).astype(BF16)
    c_map = jnp.concatenate([out_map(c_re), out_map(-c_im)], axis=1).astype(BF16)

    pw = jnp.stack([pw_re, pw_im], axis=1).reshape(SUBLANES, 2, N_SG, SG_STATE)
    r = jnp.arange(SUBLANES)
    kinds = [jnp.where((r >= s)[None, None, :, None], pw[s - 1].transpose(1, 0, 2)[:, :, None, :], 0.0)
             for s in (1, 2, 4)]
    kinds.append(pw.transpose(2, 1, 0, 3))
    tables = jnp.stack(kinds, axis=1)
    return b_map, c_map, tables


def _ssm_kernel(u_ref, bmap_ref, cmap_ref, tab_ref, d_ref, gw_ref, gb_ref, o_ref,
                hbuf, ybuf, carry_ref, *, t_rows):
    @pl.when(pl.program_id(1) == 0)
    def _():
        carry_ref[...] = jnp.zeros(carry_ref.shape, F32)

    def fma(xr, xi, ar, ai, sr, si):
        return xr + ar * sr - ai * si, xi + ar * si + ai * sr

    for sg in range(N_SG):
        u_sg = u_ref[:, sg * SG_IN:(sg + 1) * SG_IN]
        hbuf[...] = jnp.dot(u_sg, bmap_ref[sg], preferred_element_type=F32)

        def row_block(i, carry, sg=sg):
            cr, ci = carry
            rs = pl.ds(pl.multiple_of(i * SUBLANES, SUBLANES), SUBLANES)
            xr = hbuf[rs, 0:SG_STATE]
            xi = hbuf[rs, SG_STATE:2 * SG_STATE]
            for kind, shift in enumerate((1, 2, 4)):
                sr = pltpu.roll(xr, shift, axis=0)
                si = pltpu.roll(xi, shift, axis=0)
                xr, xi = fma(xr, xi, tab_ref[sg, kind, 0], tab_ref[sg, kind, 1], sr, si)
            xr, xi = fma(xr, xi, tab_ref[sg, 3, 0], tab_ref[sg, 3, 1], cr, ci)
            hbuf[rs, 0:SG_STATE] = xr
            hbuf[rs, SG_STATE:2 * SG_STATE] = xi
            last = SUBLANES - 1
            return (jnp.broadcast_to(xr[last:last + 1, :], xr.shape),
                    jnp.broadcast_to(xi[last:last + 1, :], xi.shape))

        cr, ci = lax.fori_loop(0, t_rows // SUBLANES, row_block, (carry_ref[sg, 0], carry_ref[sg, 1]))
        carry_ref[sg, 0] = cr
        carry_ref[sg, 1] = ci
        ybuf[:, sg * SG_IN:(sg + 1) * SG_IN] = jnp.dot(hbuf[...].astype(BF16), cmap_ref[sg],
                                                       preferred_element_type=F32)

    y = ybuf[...] + d_ref[...] * u_ref[...].astype(F32)
    y = 0.5 * y * (1.0 + jnp.tanh(math.sqrt(2.0 / math.pi) * (y + 0.044715 * (y * y * y))))
    gate = jnp.dot(y.astype(BF16), gw_ref[...], preferred_element_type=F32) + gb_ref[...]
    o_ref[...] = (y * jax.nn.sigmoid(gate)).astype(o_ref.dtype)


def s5_ssm(proj, b_map, c_map, tables, d_skip, glu_w, glu_b, *, batch, seq, t_rows=256):
    m = batch * seq
    nt = seq // t_rows
    const = lambda shape: pl.BlockSpec(shape, lambda b, t: (0,) * len(shape))
    return pl.pallas_call(
        functools.partial(_ssm_kernel, t_rows=t_rows),
        grid=(batch, nt),
        in_specs=[pl.BlockSpec((t_rows, SSM_WIDTH), lambda b, t: (b * nt + t, COL_C_IN // SSM_WIDTH)),
                  const(b_map.shape), const(c_map.shape), const(tables.shape),
                  const((1, SSM_WIDTH)), const((SSM_WIDTH, SSM_WIDTH)), const((1, SSM_WIDTH))],
        out_specs=pl.BlockSpec((t_rows, SSM_WIDTH), lambda b, t: (b * nt + t, 0)),
        out_shape=jax.ShapeDtypeStruct((m, SSM_WIDTH), BF16),
        scratch_shapes=[pltpu.VMEM((t_rows, 2 * SG_STATE), F32),
                        pltpu.VMEM((t_rows, SSM_WIDTH), F32),
                        pltpu.VMEM((N_SG, 2, SUBLANES, SG_STATE), F32)],
        compiler_params=_params("parallel", "arbitrary"),
        name="s5_ssm",
    )(proj, b_map, c_map, tables, d_skip.reshape(1, -1), glu_w, glu_b.reshape(1, -1))


def _silu(x):
    return x * jax.nn.sigmoid(x)


def _branch_kernel(ya_ref, ga_ref, yb_ref, gb0_ref, gb1_ref, yc_ref, gc_ref, g_ref, o_ref):
    a0, b0, c0 = 0, CONV_WIDTH, CONV_WIDTH + SB_WIDTH
    ya = ya_ref[...].astype(F32) * _silu(ga_ref[...].astype(F32))
    o_ref[:, a0:b0] = _rms(ya, g_ref[:, a0:b0]).astype(o_ref.dtype)
    gate_b = jnp.concatenate([gb0_ref[...], gb1_ref[...]], axis=1).astype(F32)
    yb = yb_ref[...].astype(F32) * _silu(gate_b)
    o_ref[:, b0:c0] = _rms(yb, g_ref[:, b0:c0]).astype(o_ref.dtype)
    yc = yc_ref[...].astype(F32) * _silu(gc_ref[...].astype(F32))
    o_ref[:, c0:] = _rms(yc, g_ref[:, c0:]).astype(o_ref.dtype)


def branch_gate_norm(ya, yb, yc, proj, g, *, tm=256):
    m = ya.shape[0]
    w = CONV_WIDTH
    pcol = lambda c: pl.BlockSpec((tm, w), lambda i: (i, c // w))
    return pl.pallas_call(
        _branch_kernel,
        grid=(m // tm,),
        in_specs=[pl.BlockSpec((tm, CONV_WIDTH), lambda i: (i, 0)), pcol(COL_A_GATE),
                  pl.BlockSpec((tm, SB_WIDTH), lambda i: (i, 0)), pcol(COL_B_GATE), pcol(COL_B_GATE + w),
                  pl.BlockSpec((tm, SSM_WIDTH), lambda i: (i, 0)), pcol(COL_C_GATE),
                  pl.BlockSpec((1, D_MODEL), lambda i: (0, 0))],
        out_specs=pl.BlockSpec((tm, D_MODEL), lambda i: (i, 0)),
        out_shape=jax.ShapeDtypeStruct((m, D_MODEL), BF16),
        compiler_params=_params("parallel"),
        name="branch_gate_norm",
    )(ya, proj, yb, proj, proj, yc, proj, g.reshape(1, -1))


def _xattn_kernel(q_ref, k_ref, v_ref, o_ref):
    s = lax.dot_general(q_ref[...], k_ref[...], (((1,), (1,)), ((), ())),
                        preferred_element_type=F32) / math.sqrt(XA_HEAD_DIM)
    e = jnp.exp(s - jnp.max(s, axis=-1, keepdims=True))
    p = e / jnp.sum(e, axis=-1, keepdims=True)
    o_ref[...] = jnp.dot(p.astype(BF16), v_ref[...], preferred_element_type=F32).astype(o_ref.dtype)


def memory_attention(q, k, v, *, batch, seq, mem_len, tq=512):
    m = batch * seq
    nq = seq // tq
    return pl.pallas_call(
        _xattn_kernel,
        grid=(batch, XA_HEADS, nq),
        in_specs=[pl.BlockSpec((tq, XA_HEAD_DIM), lambda b, h, i: (b * nq + i, h)),
                  pl.BlockSpec((mem_len, XA_HEAD_DIM), lambda b, h, i: (b, h)),
                  pl.BlockSpec((mem_len, XA_HEAD_DIM), lambda b, h, i: (b, h))],
        out_specs=pl.BlockSpec((tq, XA_HEAD_DIM), lambda b, h, i: (b * nq + i, h)),
        out_shape=jax.ShapeDtypeStruct((m, D_MODEL), BF16),
        compiler_params=_params("parallel", "parallel", "arbitrary"),
        name="memory_attention",
    )(q, k, v)


def kernel(x, mem, pre_norm_g, w_in, conv_w, conv_b, conv_ln_g, conv_ln_b, ssm_lambda_re, ssm_lambda_im, ssm_log_dt, ssm_b_re, ssm_b_im, ssm_c_re, ssm_c_im, ssm_d, ssm_glu_w, ssm_glu_b, branch_norm_g, w_out, post_norm_g, xa_pre_g, xa_mem_g, xa_wq, xa_wk, xa_wv, xa_wo, xa_post_g):
    batch, seq, d = x.shape
    mem_len = mem.shape[1]
    depth = w_in.shape[0]
    xs = x.reshape(batch * seq, d)
    mems = mem.reshape(batch * mem_len, d)

    h = rmsnorm_cast(xs, pre_norm_g[0])
    for l in range(depth):
        proj = matmul(h, w_in, l, tm=1024, tn=512, name="in_proj")
        ya = conformer_conv(proj, conv_w[l], conv_b[l], conv_ln_g[l], conv_ln_b[l], batch=batch, seq=seq)
        yb = stick_breaking_attention(proj, batch=batch, seq=seq)
        b_map, c_map, tables = ssm_params(ssm_lambda_re[l], ssm_lambda_im[l], ssm_log_dt[l],
                                          ssm_b_re[l], ssm_b_im[l], ssm_c_re[l], ssm_c_im[l])
        yc = s5_ssm(proj, b_map, c_map, tables, ssm_d[l], ssm_glu_w[l].astype(BF16), ssm_glu_b[l],
                    batch=batch, seq=seq)
        y = branch_gate_norm(ya, yb, yc, proj, branch_norm_g[l])
        o = matmul(y, w_out, l, tm=1024, tn=512, name="out_proj")
        xs, h2 = residual_norm(o, xs, post_norm_g[l], xa_pre_g[l])

        mn = rmsnorm_cast(mems, xa_mem_g[l])
        q = matmul(h2, xa_wq, l, tm=1024, tn=512, name="xa_q")
        k = matmul(mn, xa_wk, l, tm=1024, tn=512, name="xa_k")
        v = matmul(mn, xa_wv, l, tm=1024, tn=512, name="xa_v")
        a = memory_attention(q, k, v, batch=batch, seq=seq, mem_len=mem_len)
        o = matmul(a, xa_wo, l, tm=1024, tn=512, name="xa_o")
        g_next = pre_norm_g[l + 1] if l + 1 < depth else None
        xs, h = residual_norm(o, xs, xa_post_g[l], g_next)
    return xs.reshape(batch, seq, d)
```

```python
import functools
import math

import jax
import jax.numpy as jnp
from jax import lax
from jax.experimental import pallas as pl
from jax.experimental.pallas import tpu as pltpu

F32 = jnp.float32
BF16 = jnp.bfloat16

D_MODEL = 4096
CONV_WIDTH = 1024
SB_WIDTH = 2048
SSM_WIDTH = 1024
CONV_TAPS = 31
SB_HEAD_DIM = 128
SB_HEADS = SB_WIDTH // SB_HEAD_DIM
SB_BLOCK = 128
SSM_GROUP = 16
SSM_GROUPS = SSM_WIDTH // SSM_GROUP
SSM_STATE = 64
XA_HEADS = 4
XA_HEAD_DIM = D_MODEL // XA_HEADS
EPS = 1e-6

COL_A_VAL = 0
COL_A_GLU = CONV_WIDTH
COL_A_GATE = 2 * CONV_WIDTH
COL_Q = 3 * CONV_WIDTH
COL_K = COL_Q + SB_WIDTH
COL_V = COL_K + SB_WIDTH
COL_B_GATE = COL_V + SB_WIDTH
COL_C_IN = COL_B_GATE + SB_WIDTH
COL_C_GATE = COL_C_IN + SSM_WIDTH
IN_WIDTH = COL_C_GATE + SSM_WIDTH

LANES = 128
SUBLANES = 8
VMEM_LIMIT_BYTES = 48 * 1024 * 1024

SG_GROUPS = 16
N_SG = SSM_GROUPS // SG_GROUPS
SG_IN = SG_GROUPS * SSM_GROUP
SG_STATE = SG_GROUPS * SSM_STATE
HALO = 32


def _params(*sem):
    return pltpu.CompilerParams(dimension_semantics=sem, vmem_limit_bytes=VMEM_LIMIT_BYTES)


def _rms(x, g):
    return x * lax.rsqrt(jnp.mean(x * x, axis=-1, keepdims=True) + EPS) * g


def _rmsnorm_cast_kernel(x_ref, g_ref, o_ref):
    o_ref[...] = _rms(x_ref[...], g_ref[...]).astype(o_ref.dtype)


def rmsnorm_cast(x, g, *, tm=256):
    m, d = x.shape
    return pl.pallas_call(
        _rmsnorm_cast_kernel,
        grid=(m // tm,),
        in_specs=[pl.BlockSpec((tm, d), lambda i: (i, 0)),
                  pl.BlockSpec((1, d), lambda i: (0, 0))],
        out_specs=pl.BlockSpec((tm, d), lambda i: (i, 0)),
        out_shape=jax.ShapeDtypeStruct((m, d), BF16),
        compiler_params=_params("parallel"),
        name="rmsnorm_cast",
    )(x, g.reshape(1, d))


def _residual_norm_kernel(o_ref, x_ref, gp_ref, gn_ref, xo_ref, ho_ref):
    x1 = x_ref[...] + _rms(o_ref[...].astype(F32), gp_ref[...])
    xo_ref[...] = x1
    ho_ref[...] = _rms(x1, gn_ref[...]).astype(ho_ref.dtype)


def _residual_kernel(o_ref, x_ref, gp_ref, xo_ref):
    xo_ref[...] = x_ref[...] + _rms(o_ref[...].astype(F32), gp_ref[...])


def residual_norm(o, x, g_post, g_next, *, tm=256):
    m, d = x.shape
    row = pl.BlockSpec((tm, d), lambda i: (i, 0))
    vec = pl.BlockSpec((1, d), lambda i: (0, 0))
    if g_next is None:
        return pl.pallas_call(
            _residual_kernel, grid=(m // tm,),
            in_specs=[row, row, vec], out_specs=row,
            out_shape=jax.ShapeDtypeStruct((m, d), F32),
            compiler_params=_params("parallel"), name="residual",
        )(o, x, g_post.reshape(1, d)), None
    return pl.pallas_call(
        _residual_norm_kernel, grid=(m // tm,),
        in_specs=[row, row, vec, vec], out_specs=[row, row],
        out_shape=[jax.ShapeDtypeStruct((m, d), F32), jax.ShapeDtypeStruct((m, d), BF16)],
        compiler_params=_params("parallel"), name="residual_norm",
    )(o, x, g_post.reshape(1, d), g_next.reshape(1, d))


def _matmul_kernel(a_ref, w_ref, o_ref, wb_ref):
    @pl.when(pl.program_id(1) == 0)
    def _():
        wb_ref[...] = w_ref[...].astype(BF16)

    o_ref[...] = jnp.dot(a_ref[...], wb_ref[...], preferred_element_type=F32).astype(o_ref.dtype)


def matmul(a, w, layer, *, tm, tn, out_dtype=BF16, name="matmul"):
    m, k = a.shape
    _, _, n = w.shape
    tm = min(tm, m)
    return pl.pallas_call(
        _matmul_kernel,
        grid=(n // tn, m // tm),
        in_specs=[pl.BlockSpec((tm, k), lambda j, i: (i, 0)),
                  pl.BlockSpec((None, k, tn), lambda j, i: (layer, 0, j))],
        out_specs=pl.BlockSpec((tm, tn), lambda j, i: (i, j)),
        out_shape=jax.ShapeDtypeStruct((m, n), out_dtype),
        scratch_shapes=[pltpu.VMEM((k, tn), BF16)],
        compiler_params=_params("parallel", "arbitrary"),
        name=name,
    )(a, w)


def _conv_kernel(val_ref, glu_ref, w_ref, b_ref, g_ref, beta_ref, o_ref, ubuf, sbuf, cbuf, *, t_rows):
    @pl.when(pl.program_id(1) == 0)
    def _():
        ubuf[0:HALO, :] = jnp.zeros((HALO, CONV_WIDTH), F32)

    u = val_ref[...].astype(F32) * jax.nn.sigmoid(glu_ref[...].astype(F32))
    ubuf[HALO:HALO + t_rows, :] = u

    first = HALO - (CONV_TAPS - 1)
    span = t_rows + HALO - SUBLANES

    def lane_chunk(c, carry):
        cs = pl.ds(pl.multiple_of(c * LANES, LANES), LANES)
        for r in range(1, SUBLANES):
            sbuf[r, 0:span, :] = ubuf[pl.ds(r, span), cs]
        acc = jnp.zeros((t_rows, LANES), F32)
        for k in range(CONV_TAPS):
            a, r = divmod(first + k, SUBLANES)
            rows = pl.ds(a * SUBLANES, t_rows)
            shifted = ubuf[rows, cs] if r == 0 else sbuf[r, rows, :]
            acc = acc + w_ref[pl.ds(k, 1), cs] * shifted
        cbuf[:, cs] = acc
        return carry

    lax.fori_loop(0, CONV_WIDTH // LANES, lane_chunk, 0)
    ubuf[0:HALO, :] = ubuf[t_rows:t_rows + HALO, :]

    y = cbuf[...] + b_ref[...]
    mu = jnp.mean(y, axis=-1, keepdims=True)
    yc = y - mu
    yn = yc * lax.rsqrt(jnp.mean(yc * yc, axis=-1, keepdims=True) + EPS) * g_ref[...] + beta_ref[...]
    o_ref[...] = (yn * jax.nn.sigmoid(yn)).astype(o_ref.dtype)


def conformer_conv(proj, conv_w, conv_b, ln_g, ln_b, *, batch, seq, t_rows=256):
    m = batch * seq
    nt = seq // t_rows
    w = jnp.zeros((HALO, CONV_WIDTH), F32).at[:CONV_TAPS].set(conv_w)
    vec = pl.BlockSpec((1, CONV_WIDTH), lambda b, t: (0, 0))
    return pl.pallas_call(
        functools.partial(_conv_kernel, t_rows=t_rows),
        grid=(batch, nt),
        in_specs=[pl.BlockSpec((t_rows, CONV_WIDTH), lambda b, t: (b * nt + t, COL_A_VAL // CONV_WIDTH)),
                  pl.BlockSpec((t_rows, CONV_WIDTH), lambda b, t: (b * nt + t, COL_A_GLU // CONV_WIDTH)),
                  pl.BlockSpec((HALO, CONV_WIDTH), lambda b, t: (0, 0)),
                  vec, vec, vec],
        out_specs=pl.BlockSpec((t_rows, CONV_WIDTH), lambda b, t: (b * nt + t, 0)),
        out_shape=jax.ShapeDtypeStruct((m, CONV_WIDTH), BF16),
        scratch_shapes=[pltpu.VMEM((t_rows + HALO, CONV_WIDTH), F32),
                        pltpu.VMEM((SUBLANES, t_rows + HALO - SUBLANES, LANES), F32),
                        pltpu.VMEM((t_rows, CONV_WIDTH), F32)],
        compiler_params=_params("parallel", "arbitrary"),
        name="conformer_conv",
    )(proj, proj, w, conv_b.reshape(1, -1), ln_g.reshape(1, -1), ln_b.reshape(1, -1))


SB_TILE = 256


def _sb_kernel(q_ref, k_ref, v_ref, o_ref, q2_ref, z_ref, own_ref, w_ref, c_ref, acc_ref, *, n_heads):
    t = SB_TILE
    i = pl.program_id(2)
    log2e = 1.0 / math.log(2.0)
    row = lax.broadcasted_iota(jnp.int32, (t, t), 0)
    col = lax.broadcasted_iota(jnp.int32, (t, t), 1)
    suffix = jnp.where(row > col, 1.0, 0.0).astype(BF16)
    before = col < row

    heads = range(n_heads)
    lanes = [slice(hh * SB_HEAD_DIM, (hh + 1) * SB_HEAD_DIM) for hh in heads]

    def key_rows(j):
        return pl.ds(pl.multiple_of(j * t, t), t)

    def scores(j, hh):
        return lax.dot_general(q2_ref[hh], k_ref[key_rows(j), lanes[hh]], (((1,), (1,)), ((), ())),
                               preferred_element_type=F32)

    def weighted_values(slot, j, hh):
        return jnp.dot(w_ref[slot, hh], v_ref[key_rows(j), lanes[hh]], preferred_element_type=F32)

    def trip(j, slot, diagonal):
        drops = []
        for hh in heads:
            z = z_ref[slot, hh]
            l = jnp.log(1.0 + jnp.exp2(-jnp.abs(z))) * log2e
            sp = jnp.maximum(z, 0.0) + l
            own_ref[hh] = z - sp
            drops.append(jnp.where(before, sp, 0.0) if diagonal else sp)
        stacked = jnp.concatenate([d.astype(BF16) for d in drops], axis=0)
        inner_all = jnp.dot(stacked, suffix, preferred_element_type=F32)
        inner = [inner_all[hh * t:(hh + 1) * t] for hh in heads]
        for hh in heads:
            z_ref[1 - slot, hh] = scores(jnp.maximum(j - 1, 0), hh)
        for hh in heads:
            if diagonal:
                acc_ref[hh] = jnp.zeros((t, SB_HEAD_DIM), F32)
            else:
                acc_ref[hh] = acc_ref[hh] + weighted_values(1 - slot, j + 1, hh)
        for hh in heads:
            later = inner[hh] if diagonal else inner[hh] + jnp.concatenate([c_ref[hh]] * (t // LANES), axis=1)
            w = jnp.exp2(own_ref[hh] - later)
            w_ref[slot, hh] = (jnp.where(before, w, 0.0) if diagonal else w).astype(BF16)
            total = jnp.broadcast_to(jnp.sum(drops[hh], axis=-1, keepdims=True), (t, LANES))
            c_ref[hh] = total if diagonal else c_ref[hh] + total

    for hh in heads:
        q2 = q_ref[:, lanes[hh]].astype(F32) * (log2e / math.sqrt(SB_HEAD_DIM))
        q2_ref[hh] = q2.astype(BF16)
    for hh in heads:
        z_ref[0, hh] = scores(i, hh)
    trip(i, 0, True)

    def kv_pair(p, ca):
        j = i - 1 - 2 * p
        trip(j, 1, False)
        trip(j - 1, 0, False)
        return ca

    lax.fori_loop(0, i // 2, kv_pair, 0)

    @pl.when(i % 2 == 1)
    def _():
        trip(0, 1, False)

    for hh in heads:
        o_ref[:, lanes[hh]] = (acc_ref[hh] + weighted_values(i & 1, 0, hh)).astype(o_ref.dtype)


def stick_breaking_attention(proj, *, batch, seq, heads_per_step=8):
    m = batch * seq
    width = heads_per_step * SB_HEAD_DIM
    nq = seq // SB_TILE
    qc, kc, vc = (c // width for c in (COL_Q, COL_K, COL_V))
    q_spec = lambda c0: pl.BlockSpec((SB_TILE, width), lambda b, h, i: (b * nq + i, c0 + h))
    kv_spec = lambda c0: pl.BlockSpec((seq, width), lambda b, h, i: (b, c0 + h))
    tile = (heads_per_step, SB_TILE, SB_TILE)
    head = (heads_per_step, SB_TILE, SB_HEAD_DIM)
    return pl.pallas_call(
        functools.partial(_sb_kernel, n_heads=heads_per_step),
        grid=(batch, SB_HEADS // heads_per_step, nq),
        in_specs=[q_spec(qc), kv_spec(kc), kv_spec(vc)],
        out_specs=q_spec(0),
        out_shape=jax.ShapeDtypeStruct((m, SB_WIDTH), BF16),
        scratch_shapes=[pltpu.VMEM(head, BF16),
                        pltpu.VMEM((2,) + tile, F32),
                        pltpu.VMEM(tile, F32),
                        pltpu.VMEM((2,) + tile, BF16),
                        pltpu.VMEM((heads_per_step, SB_TILE, LANES), F32),
                        pltpu.VMEM(head, F32)],
        compiler_params=_params("parallel", "parallel", "arbitrary"),
        name="stick_breaking",
    )(proj, proj, proj)


def _cmul(ar, ai, br, bi):
    return ar * br - ai * bi, ar * bi + ai * br


def _ssm_param_kernel(lr_ref, li_ref, ldt_ref, lrr_ref, lir_ref, br_ref, bi_ref,
                      pr_ref, pi_ref, bbr_ref, bbi_ref):
    def zoh(lr, li, dt):
        mag = jnp.exp(lr * dt)
        ang = li * dt
        er, ei = mag * jnp.cos(ang), mag * jnp.sin(ang)
        nr, ni = er - 1.0, ei
        den = lr * lr + li * li
        return er, ei, (nr * lr + ni * li) / den, (ni * lr - nr * li) / den

    dt = jnp.exp(ldt_ref[...])
    er, ei, _, _ = zoh(lr_ref[...], li_ref[...], dt)
    pr, pi = er, ei
    pr_ref[0], pi_ref[0] = pr, pi
    for n in range(1, SUBLANES):
        pr, pi = _cmul(pr, pi, er, ei)
        pr_ref[n], pi_ref[n] = pr, pi

    _, _, cr, ci = zoh(lrr_ref[...], lir_ref[...], dt)
    bbr_ref[...], bbi_ref[...] = _cmul(cr, ci, br_ref[...], bi_ref[...])


def ssm_params(lam_re, lam_im, log_dt, b_re, b_im, c_re, c_im):
    g, p, c = SSM_GROUPS, SSM_STATE, SSM_GROUP
    full = lambda shape: pl.BlockSpec(shape, lambda: (0,) * len(shape))
    pw_re, pw_im, bb_re, bb_im = pl.pallas_call(
        _ssm_param_kernel,
        in_specs=[full((g, p)), full((g, p)), full((g, 1)), full((g, p * c)), full((g, p * c)),
                  full((g, p * c)), full((g, p * c))],
        out_specs=[full((SUBLANES, g, p)), full((SUBLANES, g, p)), full((g, p * c)), full((g, p * c))],
        out_shape=[jax.ShapeDtypeStruct((SUBLANES, g, p), F32)] * 2
        + [jax.ShapeDtypeStruct((g, p * c), F32)] * 2,
        name="ssm_params",
    )(lam_re, lam_im, log_dt.reshape(g, 1), jnp.repeat(lam_re, c, axis=1), jnp.repeat(lam_im, c, axis=1),
      b_re.reshape(g, p * c), b_im.reshape(g, p * c))

    eye = jnp.eye(SG_GROUPS, dtype=F32)

    def in_map(bb):
        t = bb.reshape(N_SG, SG_GROUPS, p, c).transpose(0, 1, 3, 2)
        return jnp.einsum("ab,sacp->sacbp", eye, t).reshape(N_SG, SG_IN, SG_STATE)

    def out_map(cc):
        t = cc.reshape(N_SG, SG_GROUPS, c, p).transpose(0, 1, 3, 2)
        return jnp.einsum("ab,sapc->sapbc", eye, t).reshape(N_SG, SG_STATE, SG_IN)

    b_map = jnp.concatenate([in_map(bb_re), in_map(bb_im)], axis=2).astype(BF16)
    c_map = jnp.concatenate([out_map(c_re), out_map(-c_im)], axis=1).astype(BF16)

    pw = jnp.stack([pw_re, pw_im], axis=1).reshape(SUBLANES, 2, N_SG, SG_STATE)
    r = jnp.arange(SUBLANES)
    kinds = [jnp.where((r >= s)[None, None, :, None], pw[s - 1].transpose(1, 0, 2)[:, :, None, :], 0.0)
             for s in (1, 2, 4)]
    kinds.append(pw.transpose(2, 1, 0, 3))
    tables = jnp.stack(kinds, axis=1)
    return b_map, c_map, tables


def _ssm_kernel(u_ref, bmap_ref, cmap_ref, tab_ref, d_ref, gw_ref, gb_ref, o_ref,
                hbuf, ybuf, carry_ref, *, t_rows):
    @pl.when(pl.program_id(1) == 0)
    def _():
        carry_ref[...] = jnp.zeros(carry_ref.shape, F32)

    def fma(xr, xi, ar, ai, sr, si):
        return xr + ar * sr - ai * si, xi + ar * si + ai * sr

    for sg in range(N_SG):
        u_sg = u_ref[:, sg * SG_IN:(sg + 1) * SG_IN]
        hbuf[...] = jnp.dot(u_sg, bmap_ref[sg], preferred_element_type=F32)

        def row_block(i, carry, sg=sg):
            cr, ci = carry
            rs = pl.ds(pl.multiple_of(i * SUBLANES, SUBLANES), SUBLANES)
            xr = hbuf[rs, 0:SG_STATE]
            xi = hbuf[rs, SG_STATE:2 * SG_STATE]
            for kind, shift in enumerate((1, 2, 4)):
                sr = pltpu.roll(xr, shift, axis=0)
                si = pltpu.roll(xi, shift, axis=0)
                xr, xi = fma(xr, xi, tab_ref[sg, kind, 0], tab_ref[sg, kind, 1], sr, si)
            xr, xi = fma(xr, xi, tab_ref[sg, 3, 0], tab_ref[sg, 3, 1], cr, ci)
            hbuf[rs, 0:SG_STATE] = xr
            hbuf[rs, SG_STATE:2 * SG_STATE] = xi
            last = SUBLANES - 1
            return (jnp.broadcast_to(xr[last:last + 1, :], xr.shape),
                    jnp.broadcast_to(xi[last:last + 1, :], xi.shape))

        cr, ci = lax.fori_loop(0, t_rows // SUBLANES, row_block, (carry_ref[sg, 0], carry_ref[sg, 1]))
        carry_ref[sg, 0] = cr
        carry_ref[sg, 1] = ci
        ybuf[:, sg * SG_IN:(sg + 1) * SG_IN] = jnp.dot(hbuf[...].astype(BF16), cmap_ref[sg],
                                                       preferred_element_type=F32)

    y = ybuf[...] + d_ref[...] * u_ref[...].astype(F32)
    y = 0.5 * y * (1.0 + jnp.tanh(math.sqrt(2.0 / math.pi) * (y + 0.044715 * (y * y * y))))
    gate = jnp.dot(y.astype(BF16), gw_ref[...], preferred_element_type=F32) + gb_ref[...]
    o_ref[...] = (y * jax.nn.sigmoid(gate)).astype(o_ref.dtype)


def s5_ssm(proj, b_map, c_map, tables, d_skip, glu_w, glu_b, *, batch, seq, t_rows=256):
    m = batch * seq
    nt = seq // t_rows
    const = lambda shape: pl.BlockSpec(shape, lambda b, t: (0,) * len(shape))
    return pl.pallas_call(
        functools.partial(_ssm_kernel, t_rows=t_rows),
        grid=(batch, nt),
        in_specs=[pl.BlockSpec((t_rows, SSM_WIDTH), lambda b, t: (b * nt + t, COL_C_IN // SSM_WIDTH)),
                  const(b_map.shape), const(c_map.shape), const(tables.shape),
                  const((1, SSM_WIDTH)), const((SSM_WIDTH, SSM_WIDTH)), const((1, SSM_WIDTH))],
        out_specs=pl.BlockSpec((t_rows, SSM_WIDTH), lambda b, t: (b * nt + t, 0)),
        out_shape=jax.ShapeDtypeStruct((m, SSM_WIDTH), BF16),
        scratch_shapes=[pltpu.VMEM((t_rows, 2 * SG_STATE), F32),
                        pltpu.VMEM((t_rows, SSM_WIDTH), F32),
                        pltpu.VMEM((N_SG, 2, SUBLANES, SG_STATE), F32)],
        compiler_params=_params("parallel", "arbitrary"),
        name="s5_ssm",
    )(proj, b_map, c_map, tables, d_skip.reshape(1, -1), glu_w, glu_b.reshape(1, -1))


def _silu(x):
    return x * jax.nn.sigmoid(x)


def _branch_kernel(ya_ref, ga_ref, yb_ref, gb0_ref, gb1_ref, yc_ref, gc_ref, g_ref, o_ref):
    a0, b0, c0 = 0, CONV_WIDTH, CONV_WIDTH + SB_WIDTH
    ya = ya_ref[...].astype(F32) * _silu(ga_ref[...].astype(F32))
    o_ref[:, a0:b0] = _rms(ya, g_ref[:, a0:b0]).astype(o_ref.dtype)
    gate_b = jnp.concatenate([gb0_ref[...], gb1_ref[...]], axis=1).astype(F32)
    yb = yb_ref[...].astype(F32) * _silu(gate_b)
    o_ref[:, b0:c0] = _rms(yb, g_ref[:, b0:c0]).astype(o_ref.dtype)
    yc = yc_ref[...].astype(F32) * _silu(gc_ref[...].astype(F32))
    o_ref[:, c0:] = _rms(yc, g_ref[:, c0:]).astype(o_ref.dtype)


def branch_gate_norm(ya, yb, yc, proj, g, *, tm=256):
    m = ya.shape[0]
    w = CONV_WIDTH
    pcol = lambda c: pl.BlockSpec((tm, w), lambda i: (i, c // w))
    return pl.pallas_call(
        _branch_kernel,
        grid=(m // tm,),
        in_specs=[pl.BlockSpec((tm, CONV_WIDTH), lambda i: (i, 0)), pcol(COL_A_GATE),
                  pl.BlockSpec((tm, SB_WIDTH), lambda i: (i, 0)), pcol(COL_B_GATE), pcol(COL_B_GATE + w),
                  pl.BlockSpec((tm, SSM_WIDTH), lambda i: (i, 0)), pcol(COL_C_GATE),
                  pl.BlockSpec((1, D_MODEL), lambda i: (0, 0))],
        out_specs=pl.BlockSpec((tm, D_MODEL), lambda i: (i, 0)),
        out_shape=jax.ShapeDtypeStruct((m, D_MODEL), BF16),
        compiler_params=_params("parallel"),
        name="branch_gate_norm",
    )(ya, proj, yb, proj, proj, yc, proj, g.reshape(1, -1))


def _xattn_kernel(q_ref, k_ref, v_ref, o_ref, *, n_sub):
    rows = q_ref.shape[0] // n_sub
    subs = [slice(r * rows, (r + 1) * rows) for r in range(n_sub)]
    ss = [lax.dot_general(q_ref[sl, :], k_ref[...], (((1,), (1,)), ((), ())),
                          preferred_element_type=F32) / math.sqrt(XA_HEAD_DIM) for sl in subs]
    es = [jnp.exp(s - jnp.max(s, axis=-1, keepdims=True)) for s in ss]
    ps = [e / jnp.sum(e, axis=-1, keepdims=True) for e in es]
    outs = [jnp.dot(p.astype(BF16), v_ref[...], preferred_element_type=F32) for p in ps]
    for sl, out in zip(subs, outs):
        o_ref[sl, :] = out.astype(o_ref.dtype)


def memory_attention(q, k, v, *, batch, seq, mem_len, tq=1024, n_sub=2):
    m = batch * seq
    nq = seq // tq
    return pl.pallas_call(
        functools.partial(_xattn_kernel, n_sub=n_sub),
        grid=(batch, XA_HEADS, nq),
        in_specs=[pl.BlockSpec((tq, XA_HEAD_DIM), lambda b, h, i: (b * nq + i, h)),
                  pl.BlockSpec((mem_len, XA_HEAD_DIM), lambda b, h, i: (b, h)),
                  pl.BlockSpec((mem_len, XA_HEAD_DIM), lambda b, h, i: (b, h))],
        out_specs=pl.BlockSpec((tq, XA_HEAD_DIM), lambda b, h, i: (b * nq + i, h)),
        out_shape=jax.ShapeDtypeStruct((m, D_MODEL), BF16),
        compiler_params=_params("parallel", "parallel", "arbitrary"),
        name="memory_attention",
    )(q, k, v)


def kernel(x, mem, pre_norm_g, w_in, conv_w, conv_b, conv_ln_g, conv_ln_b, ssm_lambda_re, ssm_lambda_im, ssm_log_dt, ssm_b_re, ssm_b_im, ssm_c_re, ssm_c_im, ssm_d, ssm_glu_w, ssm_glu_b, branch_norm_g, w_out, post_norm_g, xa_pre_g, xa_mem_g, xa_wq, xa_wk, xa_wv, xa_wo, xa_post_g):
    batch, seq, d = x.shape
    mem_len = mem.shape[1]
    depth = w_in.shape[0]
    xs = x.reshape(batch * seq, d)
    mems = mem.reshape(batch * mem_len, d)

    h = rmsnorm_cast(xs, pre_norm_g[0])
    for l in range(depth):
        proj = matmul(h, w_in, l, tm=1024, tn=512, name="in_proj")
        ya = conformer_conv(proj, conv_w[l], conv_b[l], conv_ln_g[l], conv_ln_b[l], batch=batch, seq=seq)
        yb = stick_breaking_attention(proj, batch=batch, seq=seq)
        b_map, c_map, tables = ssm_params(ssm_lambda_re[l], ssm_lambda_im[l], ssm_log_dt[l],
                                          ssm_b_re[l], ssm_b_im[l], ssm_c_re[l], ssm_c_im[l])
        yc = s5_ssm(proj, b_map, c_map, tables, ssm_d[l], ssm_glu_w[l].astype(BF16), ssm_glu_b[l],
                    batch=batch, seq=seq)
        y = branch_gate_norm(ya, yb, yc, proj, branch_norm_g[l])
        o = matmul(y, w_out, l, tm=1024, tn=512, name="out_proj")
        xs, h2 = residual_norm(o, xs, post_norm_g[l], xa_pre_g[l])

        mn = rmsnorm_cast(mems, xa_mem_g[l])
        q = matmul(h2, xa_wq, l, tm=1024, tn=512, name="xa_q")
        k = matmul(mn, xa_wk, l, tm=1024, tn=512, name="xa_k")
        v = matmul(mn, xa_wv, l, tm=1024, tn=512, name="xa_v")
        a = memory_attention(q, k, v, batch=batch, seq=seq, mem_len=mem_len)
        o = matmul(a, xa_wo, l, tm=1024, tn=512, name="xa_o")
        g_next = pre_norm_g[l + 1] if l + 1 < depth else None
        xs, h = residual_norm(o, xs, xa_post_g[l], g_next)
    return xs.reshape(batch, seq, d)
```

```python
import functools
import math

import jax
import jax.numpy as jnp
from jax import lax
from jax.experimental import pallas as pl
from jax.experimental.pallas import tpu as pltpu

F32 = jnp.float32
BF16 = jnp.bfloat16

D_MODEL = 4096
CONV_WIDTH = 1024
SB_WIDTH = 2048
SSM_WIDTH = 1024
CONV_TAPS = 31
SB_HEAD_DIM = 128
SB_HEADS = SB_WIDTH // SB_HEAD_DIM
SB_BLOCK = 128
SSM_GROUP = 16
SSM_GROUPS = SSM_WIDTH // SSM_GROUP
SSM_STATE = 64
XA_HEADS = 4
XA_HEAD_DIM = D_MODEL // XA_HEADS
EPS = 1e-6

COL_A_VAL = 0
COL_A_GLU = CONV_WIDTH
COL_A_GATE = 2 * CONV_WIDTH
COL_Q = 3 * CONV_WIDTH
COL_K = COL_Q + SB_WIDTH
COL_V = COL_K + SB_WIDTH
COL_B_GATE = COL_V + SB_WIDTH
COL_C_IN = COL_B_GATE + SB_WIDTH
COL_C_GATE = COL_C_IN + SSM_WIDTH
IN_WIDTH = COL_C_GATE + SSM_WIDTH

LANES = 128
SUBLANES = 8
VMEM_LIMIT_BYTES = 48 * 1024 * 1024

SG_GROUPS = 16
N_SG = SSM_GROUPS // SG_GROUPS
SG_IN = SG_GROUPS * SSM_GROUP
SG_STATE = SG_GROUPS * SSM_STATE
HALO = 32


def _params(*sem):
    return pltpu.CompilerParams(dimension_semantics=sem, vmem_limit_bytes=VMEM_LIMIT_BYTES)


def _rms(x, g):
    return x * lax.rsqrt(jnp.mean(x * x, axis=-1, keepdims=True) + EPS) * g


def _silu(x):
    return x * jax.nn.sigmoid(x)


def _rmsnorm_cast_kernel(x_ref, g_ref, o_ref):
    o_ref[...] = _rms(x_ref[...], g_ref[...]).astype(o_ref.dtype)


def rmsnorm_cast(x, g, *, tm=256):
    m, d = x.shape
    return pl.pallas_call(
        _rmsnorm_cast_kernel,
        grid=(m // tm,),
        in_specs=[pl.BlockSpec((tm, d), lambda i: (i, 0)),
                  pl.BlockSpec((1, d), lambda i: (0, 0))],
        out_specs=pl.BlockSpec((tm, d), lambda i: (i, 0)),
        out_shape=jax.ShapeDtypeStruct((m, d), BF16),
        compiler_params=_params("parallel"),
        name="rmsnorm_cast",
    )(x, g.reshape(1, d))


def _residual_norm_kernel(o_ref, x_ref, gp_ref, gn_ref, xo_ref, ho_ref):
    x1 = x_ref[...] + _rms(o_ref[...].astype(F32), gp_ref[...])
    xo_ref[...] = x1
    ho_ref[...] = _rms(x1, gn_ref[...]).astype(ho_ref.dtype)


def _residual_kernel(o_ref, x_ref, gp_ref, xo_ref):
    xo_ref[...] = x_ref[...] + _rms(o_ref[...].astype(F32), gp_ref[...])


def residual_norm(o, x, g_post, g_next, *, tm=256):
    m, d = x.shape
    row = pl.BlockSpec((tm, d), lambda i: (i, 0))
    vec = pl.BlockSpec((1, d), lambda i: (0, 0))
    if g_next is None:
        return pl.pallas_call(
            _residual_kernel, grid=(m // tm,),
            in_specs=[row, row, vec], out_specs=row,
            out_shape=jax.ShapeDtypeStruct((m, d), F32),
            compiler_params=_params("parallel"), name="residual",
        )(o, x, g_post.reshape(1, d)), None
    return pl.pallas_call(
        _residual_norm_kernel, grid=(m // tm,),
        in_specs=[row, row, vec, vec], out_specs=[row, row],
        out_shape=[jax.ShapeDtypeStruct((m, d), F32), jax.ShapeDtypeStruct((m, d), BF16)],
        compiler_params=_params("parallel"), name="residual_norm",
    )(o, x, g_post.reshape(1, d), g_next.reshape(1, d))


def _matmul_kernel(*refs):
    *a_refs, w_ref, o_ref, wb_ref = refs

    @pl.when(pl.program_id(1) == 0)
    def _():
        wb_ref[...] = w_ref[...].astype(BF16)

    a = jnp.concatenate([r[...] for r in a_refs], axis=1) if len(a_refs) > 1 else a_refs[0][...]
    o_ref[...] = jnp.dot(a, wb_ref[...], preferred_element_type=F32).astype(o_ref.dtype)


def matmul(a_parts, w, layer, *, tm, tn, out_dtype=BF16, name="matmul"):
    m = a_parts[0].shape[0]
    _, k, n = w.shape
    assert sum(a.shape[1] for a in a_parts) == k
    tm = min(tm, m)
    return pl.pallas_call(
        _matmul_kernel,
        grid=(n // tn, m // tm),
        in_specs=[pl.BlockSpec((tm, a.shape[1]), lambda j, i: (i, 0)) for a in a_parts]
        + [pl.BlockSpec((None, k, tn), lambda j, i: (layer, 0, j))],
        out_specs=pl.BlockSpec((tm, tn), lambda j, i: (i, j)),
        out_shape=jax.ShapeDtypeStruct((m, n), out_dtype),
        scratch_shapes=[pltpu.VMEM((k, tn), BF16)],
        compiler_params=_params("parallel", "arbitrary"),
        name=name,
    )(*a_parts, w)


def _conv_kernel(val_ref, glu_ref, gate_ref, w_ref, b_ref, g_ref, beta_ref, gn_ref, o_ref, ubuf, sbuf, cbuf,
                 *, t_rows):
    @pl.when(pl.program_id(1) == 0)
    def _():
        ubuf[0:HALO, :] = jnp.zeros((HALO, CONV_WIDTH), F32)

    u = val_ref[...].astype(F32) * jax.nn.sigmoid(glu_ref[...].astype(F32))
    ubuf[HALO:HALO + t_rows, :] = u

    first = HALO - (CONV_TAPS - 1)
    span = t_rows + HALO - SUBLANES

    def lane_chunk(c, carry):
        cs = pl.ds(pl.multiple_of(c * LANES, LANES), LANES)
        for r in range(1, SUBLANES):
            sbuf[r, 0:span, :] = ubuf[pl.ds(r, span), cs]
        acc = jnp.zeros((t_rows, LANES), F32)
        for k in range(CONV_TAPS):
            a, r = divmod(first + k, SUBLANES)
            rows = pl.ds(a * SUBLANES, t_rows)
            shifted = ubuf[rows, cs] if r == 0 else sbuf[r, rows, :]
            acc = acc + w_ref[pl.ds(k, 1), cs] * shifted
        cbuf[:, cs] = acc
        return carry

    lax.fori_loop(0, CONV_WIDTH // LANES, lane_chunk, 0)
    ubuf[0:HALO, :] = ubuf[t_rows:t_rows + HALO, :]

    y = cbuf[...] + b_ref[...]
    mu = jnp.mean(y, axis=-1, keepdims=True)
    yc = y - mu
    yn = yc * lax.rsqrt(jnp.mean(yc * yc, axis=-1, keepdims=True) + EPS) * g_ref[...] + beta_ref[...]
    gated = _silu(yn) * _silu(gate_ref[...].astype(F32))
    o_ref[...] = _rms(gated, gn_ref[...]).astype(o_ref.dtype)


def conformer_conv(proj, conv_w, conv_b, ln_g, ln_b, g_branch, *, batch, seq, t_rows=256):
    m = batch * seq
    nt = seq // t_rows
    w = jnp.zeros((HALO, CONV_WIDTH), F32).at[:CONV_TAPS].set(conv_w)
    vec = pl.BlockSpec((1, CONV_WIDTH), lambda b, t: (0, 0))
    pcol = lambda c: pl.BlockSpec((t_rows, CONV_WIDTH), lambda b, t: (b * nt + t, c // CONV_WIDTH))
    return pl.pallas_call(
        functools.partial(_conv_kernel, t_rows=t_rows),
        grid=(batch, nt),
        in_specs=[pcol(COL_A_VAL), pcol(COL_A_GLU), pcol(COL_A_GATE),
                  pl.BlockSpec((HALO, CONV_WIDTH), lambda b, t: (0, 0)),
                  vec, vec, vec, vec],
        out_specs=pl.BlockSpec((t_rows, CONV_WIDTH), lambda b, t: (b * nt + t, 0)),
        out_shape=jax.ShapeDtypeStruct((m, CONV_WIDTH), BF16),
        scratch_shapes=[pltpu.VMEM((t_rows + HALO, CONV_WIDTH), F32),
                        pltpu.VMEM((SUBLANES, t_rows + HALO - SUBLANES, LANES), F32),
                        pltpu.VMEM((t_rows, CONV_WIDTH), F32)],
        compiler_params=_params("parallel", "arbitrary"),
        name="conformer_conv",
    )(proj, proj, proj, w, conv_b.reshape(1, -1), ln_g.reshape(1, -1), ln_b.reshape(1, -1),
      g_branch.reshape(1, -1))


SB_TILE = 256


def _sb_kernel(q_ref, k_ref, v_ref, o_ref, q2_ref, z_ref, own_ref, w_ref, c_ref, acc_ref, *, n_heads):
    t = SB_TILE
    i = pl.program_id(2)
    log2e = 1.0 / math.log(2.0)
    row = lax.broadcasted_iota(jnp.int32, (t, t), 0)
    col = lax.broadcasted_iota(jnp.int32, (t, t), 1)
    suffix = jnp.where(row > col, 1.0, 0.0).astype(BF16)
    before = col < row

    heads = range(n_heads)
    lanes = [slice(hh * SB_HEAD_DIM, (hh + 1) * SB_HEAD_DIM) for hh in heads]

    def key_rows(j):
        return pl.ds(pl.multiple_of(j * t, t), t)

    def scores(j, hh):
        return lax.dot_general(q2_ref[hh], k_ref[key_rows(j), lanes[hh]], (((1,), (1,)), ((), ())),
                               preferred_element_type=F32)

    def weighted_values(slot, j, hh):
        return jnp.dot(w_ref[slot, hh], v_ref[key_rows(j), lanes[hh]], preferred_element_type=F32)

    def trip(j, slot, diagonal):
        drops = []
        for hh in heads:
            z = z_ref[slot, hh]
            l = jnp.log(1.0 + jnp.exp2(-jnp.abs(z))) * log2e
            sp = jnp.maximum(z, 0.0) + l
            own_ref[hh] = z - sp
            drops.append(jnp.where(before, sp, 0.0) if diagonal else sp)
        stacked = jnp.concatenate([d.astype(BF16) for d in drops], axis=0)
        inner_all = jnp.dot(stacked, suffix, preferred_element_type=F32)
        inner = [inner_all[hh * t:(hh + 1) * t] for hh in heads]
        for hh in heads:
            z_ref[1 - slot, hh] = scores(jnp.maximum(j - 1, 0), hh)
        for hh in heads:
            if diagonal:
                acc_ref[hh] = jnp.zeros((t, SB_HEAD_DIM), F32)
            else:
                acc_ref[hh] = acc_ref[hh] + weighted_values(1 - slot, j + 1, hh)
        for hh in heads:
            later = inner[hh] if diagonal else inner[hh] + jnp.concatenate([c_ref[hh]] * (t // LANES), axis=1)
            w = jnp.exp2(own_ref[hh] - later)
            w_ref[slot, hh] = (jnp.where(before, w, 0.0) if diagonal else w).astype(BF16)
            total = jnp.broadcast_to(jnp.sum(drops[hh], axis=-1, keepdims=True), (t, LANES))
            c_ref[hh] = total if diagonal else c_ref[hh] + total

    for hh in heads:
        q2 = q_ref[:, lanes[hh]].astype(F32) * (log2e / math.sqrt(SB_HEAD_DIM))
        q2_ref[hh] = q2.astype(BF16)
    for hh in heads:
        z_ref[0, hh] = scores(i, hh)
    trip(i, 0, True)

    def kv_pair(p, ca):
        j = i - 1 - 2 * p
        trip(j, 1, False)
        trip(j - 1, 0, False)
        return ca

    lax.fori_loop(0, i // 2, kv_pair, 0)

    @pl.when(i % 2 == 1)
    def _():
        trip(0, 1, False)

    for hh in heads:
        o_ref[:, lanes[hh]] = (acc_ref[hh] + weighted_values(i & 1, 0, hh)).astype(o_ref.dtype)


def stick_breaking_attention(proj, *, batch, seq, heads_per_step=8):
    m = batch * seq
    width = heads_per_step * SB_HEAD_DIM
    nq = seq // SB_TILE
    qc, kc, vc = (c // width for c in (COL_Q, COL_K, COL_V))
    q_spec = lambda c0: pl.BlockSpec((SB_TILE, width), lambda b, h, i: (b * nq + i, c0 + h))
    kv_spec = lambda c0: pl.BlockSpec((seq, width), lambda b, h, i: (b, c0 + h))
    tile = (heads_per_step, SB_TILE, SB_TILE)
    head = (heads_per_step, SB_TILE, SB_HEAD_DIM)
    return pl.pallas_call(
        functools.partial(_sb_kernel, n_heads=heads_per_step),
        grid=(batch, SB_HEADS // heads_per_step, nq),
        in_specs=[q_spec(qc), kv_spec(kc), kv_spec(vc)],
        out_specs=q_spec(0),
        out_shape=jax.ShapeDtypeStruct((m, SB_WIDTH), BF16),
        scratch_shapes=[pltpu.VMEM(head, BF16),
                        pltpu.VMEM((2,) + tile, F32),
                        pltpu.VMEM(tile, F32),
                        pltpu.VMEM((2,) + tile, BF16),
                        pltpu.VMEM((heads_per_step, SB_TILE, LANES), F32),
                        pltpu.VMEM(head, F32)],
        compiler_params=_params("parallel", "parallel", "arbitrary"),
        name="stick_breaking",
    )(proj, proj, proj)


def _cmul(ar, ai, br, bi):
    return ar * br - ai * bi, ar * bi + ai * br


def _ssm_param_kernel(lr_ref, li_ref, ldt_ref, lrr_ref, lir_ref, br_ref, bi_ref,
                      pr_ref, pi_ref, bbr_ref, bbi_ref):
    def zoh(lr, li, dt):
        mag = jnp.exp(lr * dt)
        ang = li * dt
        er, ei = mag * jnp.cos(ang), mag * jnp.sin(ang)
        nr, ni = er - 1.0, ei
        den = lr * lr + li * li
        return er, ei, (nr * lr + ni * li) / den, (ni * lr - nr * li) / den

    dt = jnp.exp(ldt_ref[...])
    er, ei, _, _ = zoh(lr_ref[...], li_ref[...], dt)
    pr_ref[...], pi_ref[...] = er, ei

    _, _, cr, ci = zoh(lrr_ref[...], lir_ref[...], dt)
    bbr_ref[...], bbi_ref[...] = _cmul(cr, ci, br_ref[...], bi_ref[...])


def ssm_params(lam_re, lam_im, log_dt, b_re, b_im, c_re, c_im):
    n_layers = lam_re.shape[0]
    g, p, c = SSM_GROUPS, SSM_STATE, SSM_GROUP
    per_layer = lambda *shape: pl.BlockSpec((None,) + shape, lambda l: (l,) + (0,) * len(shape))
    lb_re, lb_im, bb_re, bb_im = pl.pallas_call(
        _ssm_param_kernel,
        grid=(n_layers,),
        in_specs=[per_layer(g, p), per_layer(g, p), per_layer(g, 1), per_layer(g, p * c), per_layer(g, p * c),
                  per_layer(g, p * c), per_layer(g, p * c)],
        out_specs=[per_layer(g, p), per_layer(g, p), per_layer(g, p * c), per_layer(g, p * c)],
        out_shape=[jax.ShapeDtypeStruct((n_layers, g, p), F32)] * 2
        + [jax.ShapeDtypeStruct((n_layers, g, p * c), F32)] * 2,
        name="ssm_params",
    )(lam_re, lam_im, log_dt.reshape(n_layers, g, 1), jnp.repeat(lam_re, c, axis=2),
      jnp.repeat(lam_im, c, axis=2), b_re.reshape(n_layers, g, p * c), b_im.reshape(n_layers, g, p * c))

    eye = jnp.eye(SG_GROUPS, dtype=F32)

    def in_map(bb):
        t = bb.reshape(n_layers, N_SG, SG_GROUPS, p, c).transpose(0, 1, 2, 4, 3)
        return jnp.einsum("ab,lsacp->lsacbp", eye, t).reshape(n_layers, N_SG, SG_IN, SG_STATE)

    def out_map(cc):
        t = cc.reshape(n_layers, N_SG, SG_GROUPS, c, p).transpose(0, 1, 2, 4, 3)
        return jnp.einsum("ab,lsapc->lsapbc", eye, t).reshape(n_layers, N_SG, SG_STATE, SG_IN)

    b_map = jnp.concatenate([in_map(bb_re), in_map(bb_im)], axis=3).astype(BF16)
    c_map = jnp.concatenate([out_map(c_re), out_map(-c_im)], axis=2).astype(BF16)
    lam_bar = jnp.stack([lb_re, lb_im], axis=1).reshape(n_layers, 2, N_SG, SUBLANES, LANES).transpose(0, 2, 1, 3, 4)
    return b_map, c_map, lam_bar


def _ssm_kernel(u_ref, cgate_ref, bmap_ref, cmap_ref, lam_ref, d_ref, gw_ref, gb_ref, gn_ref, o_ref,
                ybuf, carry_ref, *state_refs, t_rows):
    @pl.when(pl.program_id(1) == 0)
    def _():
        carry_ref[...] = jnp.zeros(carry_ref.shape, F32)

    chunks = SG_STATE // LANES
    for sg in range(N_SG):
        u_sg = u_ref[:, sg * SG_IN:(sg + 1) * SG_IN]
        bu = jnp.dot(u_sg, bmap_ref[sg], preferred_element_type=F32)
        for part in range(2):
            for j in range(chunks):
                col = part * SG_STATE + j * LANES
                state_refs[2 * sg + part][pl.ds(j, t_rows, stride=SUBLANES), :] = bu[:, col:col + LANES]

    lam = [(lam_ref[sg, 0], lam_ref[sg, 1]) for sg in range(N_SG)]

    def step(t, carry):
        rows = pl.ds(pl.multiple_of(t * SUBLANES, SUBLANES), SUBLANES)
        out = []
        for sg in range(N_SG):
            hr, hi = carry[2 * sg], carry[2 * sg + 1]
            ar, ai = lam[sg]
            nr = ar * hr - ai * hi + state_refs[2 * sg][rows, :]
            ni = ar * hi + ai * hr + state_refs[2 * sg + 1][rows, :]
            state_refs[2 * sg][rows, :] = nr
            state_refs[2 * sg + 1][rows, :] = ni
            out += [nr, ni]
        return tuple(out)

    carry = tuple(carry_ref[k] for k in range(2 * N_SG))
    carry = lax.fori_loop(0, t_rows, step, carry, unroll=8)
    for k in range(2 * N_SG):
        carry_ref[k] = carry[k]

    for sg in range(N_SG):
        h = jnp.concatenate(
            [state_refs[2 * sg + part][pl.ds(j, t_rows, stride=SUBLANES), :].astype(BF16)
             for part in range(2) for j in range(chunks)], axis=1)
        ybuf[:, sg * SG_IN:(sg + 1) * SG_IN] = jnp.dot(h, cmap_ref[sg], preferred_element_type=F32)

    y = ybuf[...] + d_ref[...] * u_ref[...].astype(F32)
    y = 0.5 * y * (1.0 + jnp.tanh(math.sqrt(2.0 / math.pi) * (y + 0.044715 * (y * y * y))))
    gate = jnp.dot(y.astype(BF16), gw_ref[...], preferred_element_type=F32) + gb_ref[...]
    gated = y * jax.nn.sigmoid(gate) * _silu(cgate_ref[...].astype(F32))
    o_ref[...] = _rms(gated, gn_ref[...]).astype(o_ref.dtype)


def s5_ssm(proj, b_map, c_map, lam_bar, layer, d_skip, glu_w, glu_b, g_branch, *, batch, seq, t_rows=256):
    m = batch * seq
    nt = seq // t_rows
    const = lambda *shape: pl.BlockSpec(shape, lambda b, t: (0,) * len(shape))
    of_layer = lambda *shape: pl.BlockSpec((None,) + shape, lambda b, t: (layer,) + (0,) * len(shape))
    pcol = lambda c: pl.BlockSpec((t_rows, SSM_WIDTH), lambda b, t: (b * nt + t, c // SSM_WIDTH))
    return pl.pallas_call(
        functools.partial(_ssm_kernel, t_rows=t_rows),
        grid=(batch, nt),
        in_specs=[pcol(COL_C_IN), pcol(COL_C_GATE),
                  of_layer(*b_map.shape[1:]), of_layer(*c_map.shape[1:]), of_layer(*lam_bar.shape[1:]),
                  const(1, SSM_WIDTH), const(SSM_WIDTH, SSM_WIDTH), const(1, SSM_WIDTH), const(1, SSM_WIDTH)],
        out_specs=pl.BlockSpec((t_rows, SSM_WIDTH), lambda b, t: (b * nt + t, 0)),
        out_shape=jax.ShapeDtypeStruct((m, SSM_WIDTH), BF16),
        scratch_shapes=[pltpu.VMEM((t_rows, SSM_WIDTH), F32),
                        pltpu.VMEM((2 * N_SG, SUBLANES, LANES), F32)]
        + [pltpu.VMEM((t_rows * SUBLANES, LANES), F32)] * (2 * N_SG),
        compiler_params=_params("parallel", "arbitrary"),
        name="s5_ssm",
    )(proj, proj, b_map, c_map, lam_bar, d_skip.reshape(1, -1), glu_w, glu_b.reshape(1, -1),
      g_branch.reshape(1, -1))


def _sb_gate_norm_kernel(yb_ref, g0_ref, g1_ref, gn_ref, o_ref):
    gate = jnp.concatenate([g0_ref[...], g1_ref[...]], axis=1).astype(F32)
    o_ref[...] = _rms(yb_ref[...].astype(F32) * _silu(gate), gn_ref[...]).astype(o_ref.dtype)


def sb_gate_norm(yb, proj, g_branch, *, tm=512):
    m = yb.shape[0]
    half = SB_WIDTH // 2
    pcol = lambda c: pl.BlockSpec((tm, half), lambda i: (i, c // half))
    return pl.pallas_call(
        _sb_gate_norm_kernel,
        grid=(m // tm,),
        in_specs=[pl.BlockSpec((tm, SB_WIDTH), lambda i: (i, 0)), pcol(COL_B_GATE), pcol(COL_B_GATE + half),
                  pl.BlockSpec((1, SB_WIDTH), lambda i: (0, 0))],
        out_specs=pl.BlockSpec((tm, SB_WIDTH), lambda i: (i, 0)),
        out_shape=jax.ShapeDtypeStruct((m, SB_WIDTH), BF16),
        compiler_params=_params("parallel"),
        name="sb_gate_norm",
    )(yb, proj, proj, g_branch.reshape(1, -1))


def _xattn_kernel(q_ref, k_ref, v_ref, o_ref, *, n_sub):
    rows = q_ref.shape[0] // n_sub
    subs = [slice(r * rows, (r + 1) * rows) for r in range(n_sub)]
    ss = [lax.dot_general(q_ref[sl, :], k_ref[...], (((1,), (1,)), ((), ())),
                          preferred_element_type=F32) / math.sqrt(XA_HEAD_DIM) for sl in subs]
    es = [jnp.exp(s - jnp.max(s, axis=-1, keepdims=True)) for s in ss]
    ps = [e / jnp.sum(e, axis=-1, keepdims=True) for e in es]
    outs = [jnp.dot(p.astype(BF16), v_ref[...], preferred_element_type=F32) for p in ps]
    for sl, out in zip(subs, outs):
        o_ref[sl, :] = out.astype(o_ref.dtype)


def memory_attention(q, k, v, *, batch, seq, mem_len, tq=1024, n_sub=2):
    m = batch * seq
    nq = seq // tq
    return pl.pallas_call(
        functools.partial(_xattn_kernel, n_sub=n_sub),
        grid=(batch, XA_HEADS, nq),
        in_specs=[pl.BlockSpec((tq, XA_HEAD_DIM), lambda b, h, i: (b * nq + i, h)),
                  pl.BlockSpec((mem_len, XA_HEAD_DIM), lambda b, h, i: (b, h)),
                  pl.BlockSpec((mem_len, XA_HEAD_DIM), lambda b, h, i: (b, h))],
        out_specs=pl.BlockSpec((tq, XA_HEAD_DIM), lambda b, h, i: (b * nq + i, h)),
        out_shape=jax.ShapeDtypeStruct((m, D_MODEL), BF16),
        compiler_params=_params("parallel", "parallel", "arbitrary"),
        name="memory_attention",
    )(q, k, v)


def kernel(x, mem, pre_norm_g, w_in, conv_w, conv_b, conv_ln_g, conv_ln_b, ssm_lambda_re, ssm_lambda_im, ssm_log_dt, ssm_b_re, ssm_b_im, ssm_c_re, ssm_c_im, ssm_d, ssm_glu_w, ssm_glu_b, branch_norm_g, w_out, post_norm_g, xa_pre_g, xa_mem_g, xa_wq, xa_wk, xa_wv, xa_wo, xa_post_g):
    batch, seq, d = x.shape
    mem_len = mem.shape[1]
    depth = w_in.shape[0]
    xs = x.reshape(batch * seq, d)
    mems = mem.reshape(batch * mem_len, d)

    b_map, c_map, lam_bar = ssm_params(ssm_lambda_re, ssm_lambda_im, ssm_log_dt,
                                       ssm_b_re, ssm_b_im, ssm_c_re, ssm_c_im)
    glu_w = ssm_glu_w.astype(BF16)
    ga, gb, gc = (branch_norm_g[:, :CONV_WIDTH], branch_norm_g[:, CONV_WIDTH:CONV_WIDTH + SB_WIDTH],
                  branch_norm_g[:, CONV_WIDTH + SB_WIDTH:])

    h = rmsnorm_cast(xs, pre_norm_g[0])
    for l in range(depth):
        proj = matmul([h], w_in, l, tm=1024, tn=512, name="in_proj")
        ya = conformer_conv(proj, conv_w[l], conv_b[l], conv_ln_g[l], conv_ln_b[l], ga[l], batch=batch, seq=seq)
        yb = sb_gate_norm(stick_breaking_attention(proj, batch=batch, seq=seq), proj, gb[l])
        yc = s5_ssm(proj, b_map, c_map, lam_bar, l, ssm_d[l], glu_w[l], ssm_glu_b[l], gc[l],
                    batch=batch, seq=seq)
        o = matmul([ya, yb, yc], w_out, l, tm=1024, tn=512, name="out_proj")
        xs, h2 = residual_norm(o, xs, post_norm_g[l], xa_pre_g[l])

        mn = rmsnorm_cast(mems, xa_mem_g[l])
        q = matmul([h2], xa_wq, l, tm=1024, tn=512, name="xa_q")
        k = matmul([mn], xa_wk, l, tm=1024, tn=512, name="xa_k")
        v = matmul([mn], xa_wv, l, tm=1024, tn=512, name="xa_v")
        a = memory_attention(q, k, v, batch=batch, seq=seq, mem_len=mem_len)
        o = matmul([a], xa_wo, l, tm=1024, tn=512, name="xa_o")
        g_next = pre_norm_g[l + 1] if l + 1 < depth else None
        xs, h = residual_norm(o, xs, xa_post_g[l], g_next)
    return xs.reshape(batch, seq, d)
```

```python
import functools
import math

import jax
import jax.numpy as jnp
from jax import lax
from jax.experimental import pallas as pl
from jax.experimental.pallas import tpu as pltpu

F32 = jnp.float32
BF16 = jnp.bfloat16

D_MODEL = 4096
CONV_WIDTH = 1024
SB_WIDTH = 2048
SSM_WIDTH = 1024
CONV_TAPS = 31
SB_HEAD_DIM = 128
SB_HEADS = SB_WIDTH // SB_HEAD_DIM
SB_BLOCK = 128
SSM_GROUP = 16
SSM_GROUPS = SSM_WIDTH // SSM_GROUP
SSM_STATE = 64
XA_HEADS = 4
XA_HEAD_DIM = D_MODEL // XA_HEADS
EPS = 1e-6

COL_A_VAL = 0
COL_A_GLU = CONV_WIDTH
COL_A_GATE = 2 * CONV_WIDTH
COL_Q = 3 * CONV_WIDTH
COL_K = COL_Q + SB_WIDTH
COL_V = COL_K + SB_WIDTH
COL_B_GATE = COL_V + SB_WIDTH
COL_C_IN = COL_B_GATE + SB_WIDTH
COL_C_GATE = COL_C_IN + SSM_WIDTH
IN_WIDTH = COL_C_GATE + SSM_WIDTH

LANES = 128
SUBLANES = 8
VMEM_LIMIT_BYTES = 48 * 1024 * 1024

SG_GROUPS = 16
N_SG = SSM_GROUPS // SG_GROUPS
SG_IN = SG_GROUPS * SSM_GROUP
SG_STATE = SG_GROUPS * SSM_STATE
HALO = 32
MM_ROWS, MM_COLS = 2048, 512
MM_VMEM_LIMIT_BYTES = 56 * 1024 * 1024


def _params(*sem, vmem_limit_bytes=VMEM_LIMIT_BYTES):
    return pltpu.CompilerParams(dimension_semantics=sem, vmem_limit_bytes=vmem_limit_bytes)


def _rms(x, g):
    return x * lax.rsqrt(jnp.mean(x * x, axis=-1, keepdims=True) + EPS) * g


def _silu(x):
    return x * jax.nn.sigmoid(x)


def _rmsnorm_cast_kernel(x_ref, g_ref, o_ref):
    o_ref[...] = _rms(x_ref[...], g_ref[...]).astype(o_ref.dtype)


def rmsnorm_cast(x, g, *, tm=256):
    m, d = x.shape
    return pl.pallas_call(
        _rmsnorm_cast_kernel,
        grid=(m // tm,),
        in_specs=[pl.BlockSpec((tm, d), lambda i: (i, 0)),
                  pl.BlockSpec((1, d), lambda i: (0, 0))],
        out_specs=pl.BlockSpec((tm, d), lambda i: (i, 0)),
        out_shape=jax.ShapeDtypeStruct((m, d), BF16),
        compiler_params=_params("parallel"),
        name="rmsnorm_cast",
    )(x, g.reshape(1, d))


def _residual_norm_kernel(o_ref, x_ref, gp_ref, gn_ref, xo_ref, ho_ref):
    x1 = x_ref[...] + _rms(o_ref[...].astype(F32), gp_ref[...])
    xo_ref[...] = x1
    ho_ref[...] = _rms(x1, gn_ref[...]).astype(ho_ref.dtype)


def _residual_kernel(o_ref, x_ref, gp_ref, xo_ref):
    xo_ref[...] = x_ref[...] + _rms(o_ref[...].astype(F32), gp_ref[...])


def residual_norm(o, x, g_post, g_next, *, tm=256):
    m, d = x.shape
    row = pl.BlockSpec((tm, d), lambda i: (i, 0))
    vec = pl.BlockSpec((1, d), lambda i: (0, 0))
    if g_next is None:
        return pl.pallas_call(
            _residual_kernel, grid=(m // tm,),
            in_specs=[row, row, vec], out_specs=row,
            out_shape=jax.ShapeDtypeStruct((m, d), F32),
            compiler_params=_params("parallel"), name="residual",
        )(o, x, g_post.reshape(1, d)), None
    return pl.pallas_call(
        _residual_norm_kernel, grid=(m // tm,),
        in_specs=[row, row, vec, vec], out_specs=[row, row],
        out_shape=[jax.ShapeDtypeStruct((m, d), F32), jax.ShapeDtypeStruct((m, d), BF16)],
        compiler_params=_params("parallel"), name="residual_norm",
    )(o, x, g_post.reshape(1, d), g_next.reshape(1, d))


def _matmul_kernel(*refs):
    *a_refs, w_ref, o_ref = refs
    a = jnp.concatenate([r[...] for r in a_refs], axis=1) if len(a_refs) > 1 else a_refs[0][...]
    o_ref[...] = jnp.dot(a, w_ref[...].astype(BF16), preferred_element_type=F32).astype(o_ref.dtype)


def matmul(a_parts, w, layer, *, tm, tn, out_dtype=BF16, name="matmul"):
    m = a_parts[0].shape[0]
    _, k, n = w.shape
    assert sum(a.shape[1] for a in a_parts) == k
    tm = min(tm, m)
    return pl.pallas_call(
        _matmul_kernel,
        grid=(m // tm, n // tn),
        in_specs=[pl.BlockSpec((tm, a.shape[1]), lambda i, j: (i, 0), pipeline_mode=pl.Buffered(1))
                  for a in a_parts]
        + [pl.BlockSpec((None, k, tn), lambda i, j: (layer, 0, j))],
        out_specs=pl.BlockSpec((tm, tn), lambda i, j: (i, j)),
        out_shape=jax.ShapeDtypeStruct((m, n), out_dtype),
        compiler_params=_params("parallel", "arbitrary", vmem_limit_bytes=MM_VMEM_LIMIT_BYTES),
        name=name,
    )(*a_parts, w)


def _conv_kernel(val_ref, glu_ref, gate_ref, w_ref, b_ref, g_ref, beta_ref, gn_ref, o_ref, ubuf, sbuf, cbuf,
                 *, t_rows):
    @pl.when(pl.program_id(1) == 0)
    def _():
        ubuf[0:HALO, :] = jnp.zeros((HALO, CONV_WIDTH), F32)

    u = val_ref[...].astype(F32) * jax.nn.sigmoid(glu_ref[...].astype(F32))
    ubuf[HALO:HALO + t_rows, :] = u

    first = HALO - (CONV_TAPS - 1)
    span = t_rows + HALO - SUBLANES

    def lane_chunk(c, carry):
        cs = pl.ds(pl.multiple_of(c * LANES, LANES), LANES)
        for r in range(1, SUBLANES):
            sbuf[r, 0:span, :] = ubuf[pl.ds(r, span), cs]
        acc = jnp.zeros((t_rows, LANES), F32)
        for k in range(CONV_TAPS):
            a, r = divmod(first + k, SUBLANES)
            rows = pl.ds(a * SUBLANES, t_rows)
            shifted = ubuf[rows, cs] if r == 0 else sbuf[r, rows, :]
            acc = acc + w_ref[pl.ds(k, 1), cs] * shifted
        cbuf[:, cs] = acc
        return carry

    lax.fori_loop(0, CONV_WIDTH // LANES, lane_chunk, 0)
    ubuf[0:HALO, :] = ubuf[t_rows:t_rows + HALO, :]

    y = cbuf[...] + b_ref[...]
    mu = jnp.mean(y, axis=-1, keepdims=True)
    yc = y - mu
    yn = yc * lax.rsqrt(jnp.mean(yc * yc, axis=-1, keepdims=True) + EPS) * g_ref[...] + beta_ref[...]
    gated = _silu(yn) * _silu(gate_ref[...].astype(F32))
    o_ref[...] = _rms(gated, gn_ref[...]).astype(o_ref.dtype)


def conformer_conv(proj, conv_w, conv_b, ln_g, ln_b, g_branch, *, batch, seq, t_rows=256):
    m = batch * seq
    nt = seq // t_rows
    w = jnp.zeros((HALO, CONV_WIDTH), F32).at[:CONV_TAPS].set(conv_w)
    vec = pl.BlockSpec((1, CONV_WIDTH), lambda b, t: (0, 0))
    pcol = lambda c: pl.BlockSpec((t_rows, CONV_WIDTH), lambda b, t: (b * nt + t, c // CONV_WIDTH))
    return pl.pallas_call(
        functools.partial(_conv_kernel, t_rows=t_rows),
        grid=(batch, nt),
        in_specs=[pcol(COL_A_VAL), pcol(COL_A_GLU), pcol(COL_A_GATE),
                  pl.BlockSpec((HALO, CONV_WIDTH), lambda b, t: (0, 0)),
                  vec, vec, vec, vec],
        out_specs=pl.BlockSpec((t_rows, CONV_WIDTH), lambda b, t: (b * nt + t, 0)),
        out_shape=jax.ShapeDtypeStruct((m, CONV_WIDTH), BF16),
        scratch_shapes=[pltpu.VMEM((t_rows + HALO, CONV_WIDTH), F32),
                        pltpu.VMEM((SUBLANES, t_rows + HALO - SUBLANES, LANES), F32),
                        pltpu.VMEM((t_rows, CONV_WIDTH), F32)],
        compiler_params=_params("parallel", "arbitrary"),
        name="conformer_conv",
    )(proj, proj, proj, w, conv_b.reshape(1, -1), ln_g.reshape(1, -1), ln_b.reshape(1, -1),
      g_branch.reshape(1, -1))


SB_TILE = 256


def _sb_kernel(q_ref, k_ref, v_ref, o_ref, q2_ref, z_ref, own_ref, w_ref, c_ref, acc_ref, *, n_heads):
    t = SB_TILE
    i = pl.program_id(2)
    log2e = 1.0 / math.log(2.0)
    row = lax.broadcasted_iota(jnp.int32, (t, t), 0)
    col = lax.broadcasted_iota(jnp.int32, (t, t), 1)
    suffix = jnp.where(row > col, 1.0, 0.0).astype(BF16)
    before = col < row

    heads = range(n_heads)
    lanes = [slice(hh * SB_HEAD_DIM, (hh + 1) * SB_HEAD_DIM) for hh in heads]

    def key_rows(j):
        return pl.ds(pl.multiple_of(j * t, t), t)

    def scores(j, hh):
        return lax.dot_general(q2_ref[hh], k_ref[key_rows(j), lanes[hh]], (((1,), (1,)), ((), ())),
                               preferred_element_type=F32)

    def weighted_values(slot, j, hh):
        return jnp.dot(w_ref[slot, hh], v_ref[key_rows(j), lanes[hh]], preferred_element_type=F32)

    def trip(j, slot, diagonal):
        drops = []
        for hh in heads:
            z = z_ref[slot, hh]
            l = jnp.log(1.0 + jnp.exp2(-jnp.abs(z))) * log2e
            sp = jnp.maximum(z, 0.0) + l
            own_ref[hh] = z - sp
            drops.append(jnp.where(before, sp, 0.0) if diagonal else sp)
        stacked = jnp.concatenate([d.astype(BF16) for d in drops], axis=0)
        inner_all = jnp.dot(stacked, suffix, preferred_element_type=F32)
        inner = [inner_all[hh * t:(hh + 1) * t] for hh in heads]
        for hh in heads:
            z_ref[1 - slot, hh] = scores(jnp.maximum(j - 1, 0), hh)
        for hh in heads:
            if diagonal:
                acc_ref[hh] = jnp.zeros((t, SB_HEAD_DIM), F32)
            else:
                acc_ref[hh] = acc_ref[hh] + weighted_values(1 - slot, j + 1, hh)
        for hh in heads:
            later = inner[hh] if diagonal else inner[hh] + jnp.concatenate([c_ref[hh]] * (t // LANES), axis=1)
            w = jnp.exp2(own_ref[hh] - later)
            w_ref[slot, hh] = (jnp.where(before, w, 0.0) if diagonal else w).astype(BF16)
            total = jnp.broadcast_to(jnp.sum(drops[hh], axis=-1, keepdims=True), (t, LANES))
            c_ref[hh] = total if diagonal else c_ref[hh] + total

    for hh in heads:
        q2 = q_ref[:, lanes[hh]].astype(F32) * (log2e / math.sqrt(SB_HEAD_DIM))
        q2_ref[hh] = q2.astype(BF16)
    for hh in heads:
        z_ref[0, hh] = scores(i, hh)
    trip(i, 0, True)

    def kv_pair(p, ca):
        j = i - 1 - 2 * p
        trip(j, 1, False)
        trip(j - 1, 0, False)
        return ca

    lax.fori_loop(0, i // 2, kv_pair, 0)

    @pl.when(i % 2 == 1)
    def _():
        trip(0, 1, False)

    for hh in heads:
        o_ref[:, lanes[hh]] = (acc_ref[hh] + weighted_values(i & 1, 0, hh)).astype(o_ref.dtype)


def stick_breaking_attention(proj, *, batch, seq, heads_per_step=8):
    m = batch * seq
    width = heads_per_step * SB_HEAD_DIM
    nq = seq // SB_TILE
    qc, kc, vc = (c // width for c in (COL_Q, COL_K, COL_V))
    q_spec = lambda c0: pl.BlockSpec((SB_TILE, width), lambda b, h, i: (b * nq + i, c0 + h))
    kv_spec = lambda c0: pl.BlockSpec((seq, width), lambda b, h, i: (b, c0 + h))
    tile = (heads_per_step, SB_TILE, SB_TILE)
    head = (heads_per_step, SB_TILE, SB_HEAD_DIM)
    return pl.pallas_call(
        functools.partial(_sb_kernel, n_heads=heads_per_step),
        grid=(batch, SB_HEADS // heads_per_step, nq),
        in_specs=[q_spec(qc), kv_spec(kc), kv_spec(vc)],
        out_specs=q_spec(0),
        out_shape=jax.ShapeDtypeStruct((m, SB_WIDTH), BF16),
        scratch_shapes=[pltpu.VMEM(head, BF16),
                        pltpu.VMEM((2,) + tile, F32),
                        pltpu.VMEM(tile, F32),
                        pltpu.VMEM((2,) + tile, BF16),
                        pltpu.VMEM((heads_per_step, SB_TILE, LANES), F32),
                        pltpu.VMEM(head, F32)],
        compiler_params=_params("parallel", "parallel", "arbitrary"),
        name="stick_breaking",
    )(proj, proj, proj)


def _cmul(ar, ai, br, bi):
    return ar * br - ai * bi, ar * bi + ai * br


def _ssm_param_kernel(lr_ref, li_ref, ldt_ref, lrr_ref, lir_ref, br_ref, bi_ref,
                      pr_ref, pi_ref, bbr_ref, bbi_ref):
    def zoh(lr, li, dt):
        mag = jnp.exp(lr * dt)
        ang = li * dt
        er, ei = mag * jnp.cos(ang), mag * jnp.sin(ang)
        nr, ni = er - 1.0, ei
        den = lr * lr + li * li
        return er, ei, (nr * lr + ni * li) / den, (ni * lr - nr * li) / den

    dt = jnp.exp(ldt_ref[...])
    er, ei, _, _ = zoh(lr_ref[...], li_ref[...], dt)
    pr_ref[...], pi_ref[...] = er, ei

    _, _, cr, ci = zoh(lrr_ref[...], lir_ref[...], dt)
    bbr_ref[...], bbi_ref[...] = _cmul(cr, ci, br_ref[...], bi_ref[...])


def ssm_params(lam_re, lam_im, log_dt, b_re, b_im, c_re, c_im):
    n_layers = lam_re.shape[0]
    g, p, c = SSM_GROUPS, SSM_STATE, SSM_GROUP
    per_layer = lambda *shape: pl.BlockSpec((None,) + shape, lambda l: (l,) + (0,) * len(shape))
    lb_re, lb_im, bb_re, bb_im = pl.pallas_call(
        _ssm_param_kernel,
        grid=(n_layers,),
        in_specs=[per_layer(g, p), per_layer(g, p), per_layer(g, 1), per_layer(g, p * c), per_layer(g, p * c),
                  per_layer(g, p * c), per_layer(g, p * c)],
        out_specs=[per_layer(g, p), per_layer(g, p), per_layer(g, p * c), per_layer(g, p * c)],
        out_shape=[jax.ShapeDtypeStruct((n_layers, g, p), F32)] * 2
        + [jax.ShapeDtypeStruct((n_layers, g, p * c), F32)] * 2,
        name="ssm_params",
    )(lam_re, lam_im, log_dt.reshape(n_layers, g, 1), jnp.repeat(lam_re, c, axis=2),
      jnp.repeat(lam_im, c, axis=2), b_re.reshape(n_layers, g, p * c), b_im.reshape(n_layers, g, p * c))

    eye = jnp.eye(SG_GROUPS, dtype=F32)

    def in_map(bb):
        t = bb.reshape(n_layers, N_SG, SG_GROUPS, p, c).transpose(0, 1, 2, 4, 3)
        return jnp.einsum("ab,lsacp->lsacbp", eye, t).reshape(n_layers, N_SG, SG_IN, SG_STATE)

    def out_map(cc):
        t = cc.reshape(n_layers, N_SG, SG_GROUPS, c, p).transpose(0, 1, 2, 4, 3)
        return jnp.einsum("ab,lsapc->lsapbc", eye, t).reshape(n_layers, N_SG, SG_STATE, SG_IN)

    b_map = jnp.concatenate([in_map(bb_re), in_map(bb_im)], axis=3).astype(BF16)
    c_map = jnp.concatenate([out_map(c_re), out_map(-c_im)], axis=2).astype(BF16)
    lam_bar = jnp.stack([lb_re, lb_im], axis=1).reshape(n_layers, 2, N_SG, SUBLANES, LANES).transpose(0, 2, 1, 3, 4)
    return b_map, c_map, lam_bar


def _ssm_kernel(u_ref, cgate_ref, bmap_ref, cmap_ref, lam_ref, d_ref, gw_ref, gb_ref, gn_ref, o_ref,
                ybuf, carry_ref, *state_refs, t_rows):
    @pl.when(pl.program_id(1) == 0)
    def _():
        carry_ref[...] = jnp.zeros(carry_ref.shape, F32)

    chunks = SG_STATE // LANES
    for sg in range(N_SG):
        u_sg = u_ref[:, sg * SG_IN:(sg + 1) * SG_IN]
        bu = jnp.dot(u_sg, bmap_ref[sg], preferred_element_type=F32)
        for part in range(2):
            for j in range(chunks):
                col = part * SG_STATE + j * LANES
                state_refs[2 * sg + part][pl.ds(j, t_rows, stride=SUBLANES), :] = bu[:, col:col + LANES]

    lam = [(lam_ref[sg, 0], lam_ref[sg, 1]) for sg in range(N_SG)]

    def step(t, carry):
        rows = pl.ds(pl.multiple_of(t * SUBLANES, SUBLANES), SUBLANES)
        out = []
        for sg in range(N_SG):
            hr, hi = carry[2 * sg], carry[2 * sg + 1]
            ar, ai = lam[sg]
            nr = ar * hr - ai * hi + state_refs[2 * sg][rows, :]
            ni = ar * hi + ai * hr + state_refs[2 * sg + 1][rows, :]
            state_refs[2 * sg][rows, :] = nr
            state_refs[2 * sg + 1][rows, :] = ni
            out += [nr, ni]
        return tuple(out)

    carry = tuple(carry_ref[k] for k in range(2 * N_SG))
    carry = lax.fori_loop(0, t_rows, step, carry, unroll=8)
    for k in range(2 * N_SG):
        carry_ref[k] = carry[k]

    for sg in range(N_SG):
        h = jnp.concatenate(
            [state_refs[2 * sg + part][pl.ds(j, t_rows, stride=SUBLANES), :].astype(BF16)
             for part in range(2) for j in range(chunks)], axis=1)
        ybuf[:, sg * SG_IN:(sg + 1) * SG_IN] = jnp.dot(h, cmap_ref[sg], preferred_element_type=F32)

    y = ybuf[...] + d_ref[...] * u_ref[...].astype(F32)
    y = 0.5 * y * (1.0 + jnp.tanh(math.sqrt(2.0 / math.pi) * (y + 0.044715 * (y * y * y))))
    gate = jnp.dot(y.astype(BF16), gw_ref[...], preferred_element_type=F32) + gb_ref[...]
    gated = y * jax.nn.sigmoid(gate) * _silu(cgate_ref[...].astype(F32))
    o_ref[...] = _rms(gated, gn_ref[...]).astype(o_ref.dtype)


def s5_ssm(proj, b_map, c_map, lam_bar, layer, d_skip, glu_w, glu_b, g_branch, *, batch, seq, t_rows=256):
    m = batch * seq
    nt = seq // t_rows
    const = lambda *shape: pl.BlockSpec(shape, lambda b, t: (0,) * len(shape))
    of_layer = lambda *shape: pl.BlockSpec((None,) + shape, lambda b, t: (layer,) + (0,) * len(shape))
    pcol = lambda c: pl.BlockSpec((t_rows, SSM_WIDTH), lambda b, t: (b * nt + t, c // SSM_WIDTH))
    return pl.pallas_call(
        functools.partial(_ssm_kernel, t_rows=t_rows),
        grid=(batch, nt),
        in_specs=[pcol(COL_C_IN), pcol(COL_C_GATE),
                  of_layer(*b_map.shape[1:]), of_layer(*c_map.shape[1:]), of_layer(*lam_bar.shape[1:]),
                  const(1, SSM_WIDTH), const(SSM_WIDTH, SSM_WIDTH), const(1, SSM_WIDTH), const(1, SSM_WIDTH)],
        out_specs=pl.BlockSpec((t_rows, SSM_WIDTH), lambda b, t: (b * nt + t, 0)),
        out_shape=jax.ShapeDtypeStruct((m, SSM_WIDTH), BF16),
        scratch_shapes=[pltpu.VMEM((t_rows, SSM_WIDTH), F32),
                        pltpu.VMEM((2 * N_SG, SUBLANES, LANES), F32)]
        + [pltpu.VMEM((t_rows * SUBLANES, LANES), F32)] * (2 * N_SG),
        compiler_params=_params("parallel", "arbitrary"),
        name="s5_ssm",
    )(proj, proj, b_map, c_map, lam_bar, d_skip.reshape(1, -1), glu_w, glu_b.reshape(1, -1),
      g_branch.reshape(1, -1))


def _sb_gate_norm_kernel(yb_ref, g0_ref, g1_ref, gn_ref, o_ref):
    gate = jnp.concatenate([g0_ref[...], g1_ref[...]], axis=1).astype(F32)
    o_ref[...] = _rms(yb_ref[...].astype(F32) * _silu(gate), gn_ref[...]).astype(o_ref.dtype)


def sb_gate_norm(yb, proj, g_branch, *, tm=512):
    m = yb.shape[0]
    half = SB_WIDTH // 2
    pcol = lambda c: pl.BlockSpec((tm, half), lambda i: (i, c // half))
    return pl.pallas_call(
        _sb_gate_norm_kernel,
        grid=(m // tm,),
        in_specs=[pl.BlockSpec((tm, SB_WIDTH), lambda i: (i, 0)), pcol(COL_B_GATE), pcol(COL_B_GATE + half),
                  pl.BlockSpec((1, SB_WIDTH), lambda i: (0, 0))],
        out_specs=pl.BlockSpec((tm, SB_WIDTH), lambda i: (i, 0)),
        out_shape=jax.ShapeDtypeStruct((m, SB_WIDTH), BF16),
        compiler_params=_params("parallel"),
        name="sb_gate_norm",
    )(yb, proj, proj, g_branch.reshape(1, -1))


def _xattn_kernel(q_ref, k_ref, v_ref, o_ref, *, n_sub):
    rows = q_ref.shape[0] // n_sub
    subs = [slice(r * rows, (r + 1) * rows) for r in range(n_sub)]
    ss = [lax.dot_general(q_ref[sl, :], k_ref[...], (((1,), (1,)), ((), ())),
                          preferred_element_type=F32) / math.sqrt(XA_HEAD_DIM) for sl in subs]
    es = [jnp.exp(s - jnp.max(s, axis=-1, keepdims=True)) for s in ss]
    ps = [e / jnp.sum(e, axis=-1, keepdims=True) for e in es]
    outs = [jnp.dot(p.astype(BF16), v_ref[...], preferred_element_type=F32) for p in ps]
    for sl, out in zip(subs, outs):
        o_ref[sl, :] = out.astype(o_ref.dtype)


def memory_attention(q, k, v, *, batch, seq, mem_len, tq=1024, n_sub=2):
    m = batch * seq
    nq = seq // tq
    return pl.pallas_call(
        functools.partial(_xattn_kernel, n_sub=n_sub),
        grid=(batch, XA_HEADS, nq),
        in_specs=[pl.BlockSpec((tq, XA_HEAD_DIM), lambda b, h, i: (b * nq + i, h)),
                  pl.BlockSpec((mem_len, XA_HEAD_DIM), lambda b, h, i: (b, h)),
                  pl.BlockSpec((mem_len, XA_HEAD_DIM), lambda b, h, i: (b, h))],
        out_specs=pl.BlockSpec((tq, XA_HEAD_DIM), lambda b, h, i: (b * nq + i, h)),
        out_shape=jax.ShapeDtypeStruct((m, D_MODEL), BF16),
        compiler_params=_params("parallel", "parallel", "arbitrary"),
        name="memory_attention",
    )(q, k, v)


def kernel(x, mem, pre_norm_g, w_in, conv_w, conv_b, conv_ln_g, conv_ln_b, ssm_lambda_re, ssm_lambda_im, ssm_log_dt, ssm_b_re, ssm_b_im, ssm_c_re, ssm_c_im, ssm_d, ssm_glu_w, ssm_glu_b, branch_norm_g, w_out, post_norm_g, xa_pre_g, xa_mem_g, xa_wq, xa_wk, xa_wv, xa_wo, xa_post_g):
    batch, seq, d = x.shape
    mem_len = mem.shape[1]
    depth = w_in.shape[0]
    xs = x.reshape(batch * seq, d)
    mems = mem.reshape(batch * mem_len, d)

    b_map, c_map, lam_bar = ssm_params(ssm_lambda_re, ssm_lambda_im, ssm_log_dt,
                                       ssm_b_re, ssm_b_im, ssm_c_re, ssm_c_im)
    glu_w = ssm_glu_w.astype(BF16)
    ga, gb, gc = (branch_norm_g[:, :CONV_WIDTH], branch_norm_g[:, CONV_WIDTH:CONV_WIDTH + SB_WIDTH],
                  branch_norm_g[:, CONV_WIDTH + SB_WIDTH:])

    h = rmsnorm_cast(xs, pre_norm_g[0])
    for l in range(depth):
        proj = matmul([h], w_in, l, tm=MM_ROWS, tn=MM_COLS, name="in_proj")
        ya = conformer_conv(proj, conv_w[l], conv_b[l], conv_ln_g[l], conv_ln_b[l], ga[l], batch=batch, seq=seq)
        yb = sb_gate_norm(stick_breaking_attention(proj, batch=batch, seq=seq), proj, gb[l])
        yc = s5_ssm(proj, b_map, c_map, lam_bar, l, ssm_d[l], glu_w[l], ssm_glu_b[l], gc[l],
                    batch=batch, seq=seq)
        o = matmul([ya, yb, yc], w_out, l, tm=MM_ROWS, tn=MM_COLS, name="out_proj")
        xs, h2 = residual_norm(o, xs, post_norm_g[l], xa_pre_g[l])

        mn = rmsnorm_cast(mems, xa_mem_g[l])
        q = matmul([h2], xa_wq, l, tm=MM_ROWS, tn=MM_COLS, name="xa_q")
        k = matmul([mn], xa_wk, l, tm=MM_ROWS, tn=MM_COLS, name="xa_k")
        v = matmul([mn], xa_wv, l, tm=MM_ROWS, tn=MM_COLS, name="xa_v")
        a = memory_attention(q, k, v, batch=batch, seq=seq, mem_len=mem_len)
        o = matmul([a], xa_wo, l, tm=MM_ROWS, tn=MM_COLS, name="xa_o")
        g_next = pre_norm_g[l + 1] if l + 1 < depth else None
        xs, h = residual_norm(o, xs, xa_post_g[l], g_next)
    return xs.reshape(batch, seq, d)
```

```python
import functools
import math

import jax
import jax.numpy as jnp
from jax import lax
from jax.experimental import pallas as pl
from jax.experimental.pallas import tpu as pltpu

F32 = jnp.float32
BF16 = jnp.bfloat16

D_MODEL = 4096
CONV_WIDTH = 1024
SB_WIDTH = 2048
SSM_WIDTH = 1024
CONV_TAPS = 31
SB_HEAD_DIM = 128
SB_HEADS = SB_WIDTH // SB_HEAD_DIM
SB_BLOCK = 128
SSM_GROUP = 16
SSM_GROUPS = SSM_WIDTH // SSM_GROUP
SSM_STATE = 64
XA_HEADS = 4
XA_HEAD_DIM = D_MODEL // XA_HEADS
EPS = 1e-6

COL_A_VAL = 0
COL_A_GLU = CONV_WIDTH
COL_A_GATE = 2 * CONV_WIDTH
COL_Q = 3 * CONV_WIDTH
COL_K = COL_Q + SB_WIDTH
COL_V = COL_K + SB_WIDTH
COL_B_GATE = COL_V + SB_WIDTH
COL_C_IN = COL_B_GATE + SB_WIDTH
COL_C_GATE = COL_C_IN + SSM_WIDTH
IN_WIDTH = COL_C_GATE + SSM_WIDTH

LANES = 128
SUBLANES = 8
VMEM_LIMIT_BYTES = 48 * 1024 * 1024

SG_GROUPS = 16
N_SG = SSM_GROUPS // SG_GROUPS
SG_IN = SG_GROUPS * SSM_GROUP
SG_STATE = SG_GROUPS * SSM_STATE
HALO = 32
MM_ROWS, MM_COLS = 2048, 512
MM_VMEM_LIMIT_BYTES = 56 * 1024 * 1024
MM_WEIGHT_DMAS = 4


def _params(*sem, vmem_limit_bytes=VMEM_LIMIT_BYTES):
    return pltpu.CompilerParams(dimension_semantics=sem, vmem_limit_bytes=vmem_limit_bytes)


def _rms(x, g):
    return x * lax.rsqrt(jnp.mean(x * x, axis=-1, keepdims=True) + EPS) * g


def _silu(x):
    return x * jax.nn.sigmoid(x)


def _rmsnorm_cast_kernel(x_ref, g_ref, o_ref):
    o_ref[...] = _rms(x_ref[...], g_ref[...]).astype(o_ref.dtype)


def rmsnorm_cast(x, g, *, tm=256):
    m, d = x.shape
    return pl.pallas_call(
        _rmsnorm_cast_kernel,
        grid=(m // tm,),
        in_specs=[pl.BlockSpec((tm, d), lambda i: (i, 0)),
                  pl.BlockSpec((1, d), lambda i: (0, 0))],
        out_specs=pl.BlockSpec((tm, d), lambda i: (i, 0)),
        out_shape=jax.ShapeDtypeStruct((m, d), BF16),
        compiler_params=_params("parallel"),
        name="rmsnorm_cast",
    )(x, g.reshape(1, d))


def _residual_norm_kernel(o_ref, x_ref, gp_ref, gn_ref, xo_ref, ho_ref):
    x1 = x_ref[...] + _rms(o_ref[...].astype(F32), gp_ref[...])
    xo_ref[...] = x1
    ho_ref[...] = _rms(x1, gn_ref[...]).astype(ho_ref.dtype)


def _residual_kernel(o_ref, x_ref, gp_ref, xo_ref):
    xo_ref[...] = x_ref[...] + _rms(o_ref[...].astype(F32), gp_ref[...])


def residual_norm(o, x, g_post, g_next, *, tm=256):
    m, d = x.shape
    row = pl.BlockSpec((tm, d), lambda i: (i, 0))
    vec = pl.BlockSpec((1, d), lambda i: (0, 0))
    if g_next is None:
        return pl.pallas_call(
            _residual_kernel, grid=(m // tm,),
            in_specs=[row, row, vec], out_specs=row,
            out_shape=jax.ShapeDtypeStruct((m, d), F32),
            compiler_params=_params("parallel"), name="residual",
        )(o, x, g_post.reshape(1, d)), None
    return pl.pallas_call(
        _residual_norm_kernel, grid=(m // tm,),
        in_specs=[row, row, vec, vec], out_specs=[row, row],
        out_shape=[jax.ShapeDtypeStruct((m, d), F32), jax.ShapeDtypeStruct((m, d), BF16)],
        compiler_params=_params("parallel"), name="residual_norm",
    )(o, x, g_post.reshape(1, d), g_next.reshape(1, d))


def _matmul_kernel(*refs, n_a):
    a_refs, w_refs, o_ref = refs[:n_a], refs[n_a:-1], refs[-1]
    a = jnp.concatenate([r[...] for r in a_refs], axis=1) if n_a > 1 else a_refs[0][...]
    w = jnp.concatenate([r[...] for r in w_refs], axis=0)
    o_ref[...] = jnp.dot(a, w.astype(BF16), preferred_element_type=F32).astype(o_ref.dtype)


def matmul(a_parts, w, layer, *, tm, tn, out_dtype=BF16, name="matmul"):
    m = a_parts[0].shape[0]
    _, k, n = w.shape
    assert sum(a.shape[1] for a in a_parts) == k
    tm = min(tm, m)
    ks = k // MM_WEIGHT_DMAS
    w_specs = [pl.BlockSpec((None, ks, tn), functools.partial(lambda i, j, s: (layer, s, j), s=s))
               for s in range(MM_WEIGHT_DMAS)]
    return pl.pallas_call(
        functools.partial(_matmul_kernel, n_a=len(a_parts)),
        grid=(m // tm, n // tn),
        in_specs=[pl.BlockSpec((tm, a.shape[1]), lambda i, j: (i, 0), pipeline_mode=pl.Buffered(1))
                  for a in a_parts] + w_specs,
        out_specs=pl.BlockSpec((tm, tn), lambda i, j: (i, j)),
        out_shape=jax.ShapeDtypeStruct((m, n), out_dtype),
        compiler_params=_params("parallel", "arbitrary", vmem_limit_bytes=MM_VMEM_LIMIT_BYTES),
        name=name,
    )(*a_parts, *([w] * MM_WEIGHT_DMAS))


def _conv_kernel(val_ref, glu_ref, gate_ref, w_ref, b_ref, g_ref, beta_ref, gn_ref, o_ref, ubuf, sbuf, cbuf,
                 *, t_rows):
    @pl.when(pl.program_id(1) == 0)
    def _():
        ubuf[0:HALO, :] = jnp.zeros((HALO, CONV_WIDTH), F32)

    u = val_ref[...].astype(F32) * jax.nn.sigmoid(glu_ref[...].astype(F32))
    ubuf[HALO:HALO + t_rows, :] = u

    first = HALO - (CONV_TAPS - 1)
    span = t_rows + HALO - SUBLANES

    def lane_chunk(c, carry):
        cs = pl.ds(pl.multiple_of(c * LANES, LANES), LANES)
        for r in range(1, SUBLANES):
            sbuf[r, 0:span, :] = ubuf[pl.ds(r, span), cs]
        acc = jnp.zeros((t_rows, LANES), F32)
        for k in range(CONV_TAPS):
            a, r = divmod(first + k, SUBLANES)
            rows = pl.ds(a * SUBLANES, t_rows)
            shifted = ubuf[rows, cs] if r == 0 else sbuf[r, rows, :]
            acc = acc + w_ref[pl.ds(k, 1), cs] * shifted
        cbuf[:, cs] = acc
        return carry

    lax.fori_loop(0, CONV_WIDTH // LANES, lane_chunk, 0)
    ubuf[0:HALO, :] = ubuf[t_rows:t_rows + HALO, :]

    y = cbuf[...] + b_ref[...]
    mu = jnp.mean(y, axis=-1, keepdims=True)
    yc = y - mu
    yn = yc * lax.rsqrt(jnp.mean(yc * yc, axis=-1, keepdims=True) + EPS) * g_ref[...] + beta_ref[...]
    gated = _silu(yn) * _silu(gate_ref[...].astype(F32))
    o_ref[...] = _rms(gated, gn_ref[...]).astype(o_ref.dtype)


def conformer_conv(proj, conv_w, conv_b, ln_g, ln_b, g_branch, *, batch, seq, t_rows=256):
    m = batch * seq
    nt = seq // t_rows
    w = jnp.zeros((HALO, CONV_WIDTH), F32).at[:CONV_TAPS].set(conv_w)
    vec = pl.BlockSpec((1, CONV_WIDTH), lambda b, t: (0, 0))
    pcol = lambda c: pl.BlockSpec((t_rows, CONV_WIDTH), lambda b, t: (b * nt + t, c // CONV_WIDTH))
    return pl.pallas_call(
        functools.partial(_conv_kernel, t_rows=t_rows),
        grid=(batch, nt),
        in_specs=[pcol(COL_A_VAL), pcol(COL_A_GLU), pcol(COL_A_GATE),
                  pl.BlockSpec((HALO, CONV_WIDTH), lambda b, t: (0, 0)),
                  vec, vec, vec, vec],
        out_specs=pl.BlockSpec((t_rows, CONV_WIDTH), lambda b, t: (b * nt + t, 0)),
        out_shape=jax.ShapeDtypeStruct((m, CONV_WIDTH), BF16),
        scratch_shapes=[pltpu.VMEM((t_rows + HALO, CONV_WIDTH), F32),
                        pltpu.VMEM((SUBLANES, t_rows + HALO - SUBLANES, LANES), F32),
                        pltpu.VMEM((t_rows, CONV_WIDTH), F32)],
        compiler_params=_params("parallel", "arbitrary"),
        name="conformer_conv",
    )(proj, proj, proj, w, conv_b.reshape(1, -1), ln_g.reshape(1, -1), ln_b.reshape(1, -1),
      g_branch.reshape(1, -1))


SB_TILE = 256


def _sb_kernel(q_ref, k_ref, v_ref, o_ref, q2_ref, z_ref, own_ref, w_ref, c_ref, acc_ref, *, n_heads):
    t = SB_TILE
    i = pl.program_id(2)
    log2e = 1.0 / math.log(2.0)
    row = lax.broadcasted_iota(jnp.int32, (t, t), 0)
    col = lax.broadcasted_iota(jnp.int32, (t, t), 1)
    suffix = jnp.where(row > col, 1.0, 0.0).astype(BF16)
    before = col < row

    heads = range(n_heads)
    lanes = [slice(hh * SB_HEAD_DIM, (hh + 1) * SB_HEAD_DIM) for hh in heads]

    def key_rows(j):
        return pl.ds(pl.multiple_of(j * t, t), t)

    def scores(j, hh):
        return lax.dot_general(q2_ref[hh], k_ref[key_rows(j), lanes[hh]], (((1,), (1,)), ((), ())),
                               preferred_element_type=F32)

    def weighted_values(slot, j, hh):
        return jnp.dot(w_ref[slot, hh], v_ref[key_rows(j), lanes[hh]], preferred_element_type=F32)

    def trip(j, slot, diagonal):
        drops = []
        for hh in heads:
            z = z_ref[slot, hh]
            l = jnp.log(1.0 + jnp.exp2(-jnp.abs(z))) * log2e
            sp = jnp.maximum(z, 0.0) + l
            own_ref[hh] = z - sp
            drops.append(jnp.where(before, sp, 0.0) if diagonal else sp)
        stacked = jnp.concatenate([d.astype(BF16) for d in drops], axis=0)
        inner_all = jnp.dot(stacked, suffix, preferred_element_type=F32)
        inner = [inner_all[hh * t:(hh + 1) * t] for hh in heads]
        for hh in heads:
            z_ref[1 - slot, hh] = scores(jnp.maximum(j - 1, 0), hh)
        for hh in heads:
            if diagonal:
                acc_ref[hh] = jnp.zeros((t, SB_HEAD_DIM), F32)
            else:
                acc_ref[hh] = acc_ref[hh] + weighted_values(1 - slot, j + 1, hh)
        for hh in heads:
            later = inner[hh] if diagonal else inner[hh] + jnp.concatenate([c_ref[hh]] * (t // LANES), axis=1)
            w = jnp.exp2(own_ref[hh] - later)
            w_ref[slot, hh] = (jnp.where(before, w, 0.0) if diagonal else w).astype(BF16)
            total = jnp.broadcast_to(jnp.sum(drops[hh], axis=-1, keepdims=True), (t, LANES))
            c_ref[hh] = total if diagonal else c_ref[hh] + total

    for hh in heads:
        q2 = q_ref[:, lanes[hh]].astype(F32) * (log2e / math.sqrt(SB_HEAD_DIM))
        q2_ref[hh] = q2.astype(BF16)
    for hh in heads:
        z_ref[0, hh] = scores(i, hh)
    trip(i, 0, True)

    def kv_pair(p, ca):
        j = i - 1 - 2 * p
        trip(j, 1, False)
        trip(j - 1, 0, False)
        return ca

    lax.fori_loop(0, i // 2, kv_pair, 0)

    @pl.when(i % 2 == 1)
    def _():
        trip(0, 1, False)

    for hh in heads:
        o_ref[:, lanes[hh]] = (acc_ref[hh] + weighted_values(i & 1, 0, hh)).astype(o_ref.dtype)


def stick_breaking_attention(proj, *, batch, seq, heads_per_step=8):
    m = batch * seq
    width = heads_per_step * SB_HEAD_DIM
    nq = seq // SB_TILE
    qc, kc, vc = (c // width for c in (COL_Q, COL_K, COL_V))
    q_spec = lambda c0: pl.BlockSpec((SB_TILE, width), lambda b, h, i: (b * nq + i, c0 + h))
    kv_spec = lambda c0: pl.BlockSpec((seq, width), lambda b, h, i: (b, c0 + h))
    tile = (heads_per_step, SB_TILE, SB_TILE)
    head = (heads_per_step, SB_TILE, SB_HEAD_DIM)
    return pl.pallas_call(
        functools.partial(_sb_kernel, n_heads=heads_per_step),
        grid=(batch, SB_HEADS // heads_per_step, nq),
        in_specs=[q_spec(qc), kv_spec(kc), kv_spec(vc)],
        out_specs=q_spec(0),
        out_shape=jax.ShapeDtypeStruct((m, SB_WIDTH), BF16),
        scratch_shapes=[pltpu.VMEM(head, BF16),
                        pltpu.VMEM((2,) + tile, F32),
                        pltpu.VMEM(tile, F32),
                        pltpu.VMEM((2,) + tile, BF16),
                        pltpu.VMEM((heads_per_step, SB_TILE, LANES), F32),
                        pltpu.VMEM(head, F32)],
        compiler_params=_params("parallel", "parallel", "arbitrary"),
        name="stick_breaking",
    )(proj, proj, proj)


def _cmul(ar, ai, br, bi):
    return ar * br - ai * bi, ar * bi + ai * br


def _ssm_param_kernel(lr_ref, li_ref, ldt_ref, lrr_ref, lir_ref, br_ref, bi_ref,
                      pr_ref, pi_ref, bbr_ref, bbi_ref):
    def zoh(lr, li, dt):
        mag = jnp.exp(lr * dt)
        ang = li * dt
        er, ei = mag * jnp.cos(ang), mag * jnp.sin(ang)
        nr, ni = er - 1.0, ei
        den = lr * lr + li * li
        return er, ei, (nr * lr + ni * li) / den, (ni * lr - nr * li) / den

    dt = jnp.exp(ldt_ref[...])
    er, ei, _, _ = zoh(lr_ref[...], li_ref[...], dt)
    pr_ref[...], pi_ref[...] = er, ei

    _, _, cr, ci = zoh(lrr_ref[...], lir_ref[...], dt)
    bbr_ref[...], bbi_ref[...] = _cmul(cr, ci, br_ref[...], bi_ref[...])


def ssm_params(lam_re, lam_im, log_dt, b_re, b_im, c_re, c_im):
    n_layers = lam_re.shape[0]
    g, p, c = SSM_GROUPS, SSM_STATE, SSM_GROUP
    per_layer = lambda *shape: pl.BlockSpec((None,) + shape, lambda l: (l,) + (0,) * len(shape))
    lb_re, lb_im, bb_re, bb_im = pl.pallas_call(
        _ssm_param_kernel,
        grid=(n_layers,),
        in_specs=[per_layer(g, p), per_layer(g, p), per_layer(g, 1), per_layer(g, p * c), per_layer(g, p * c),
                  per_layer(g, p * c), per_layer(g, p * c)],
        out_specs=[per_layer(g, p), per_layer(g, p), per_layer(g, p * c), per_layer(g, p * c)],
        out_shape=[jax.ShapeDtypeStruct((n_layers, g, p), F32)] * 2
        + [jax.ShapeDtypeStruct((n_layers, g, p * c), F32)] * 2,
        name="ssm_params",
    )(lam_re, lam_im, log_dt.reshape(n_layers, g, 1), jnp.repeat(lam_re, c, axis=2),
      jnp.repeat(lam_im, c, axis=2), b_re.reshape(n_layers, g, p * c), b_im.reshape(n_layers, g, p * c))

    eye = jnp.eye(SG_GROUPS, dtype=F32)

    def in_map(bb):
        t = bb.reshape(n_layers, N_SG, SG_GROUPS, p, c).transpose(0, 1, 2, 4, 3)
        return jnp.einsum("ab,lsacp->lsacbp", eye, t).reshape(n_layers, N_SG, SG_IN, SG_STATE)

    def out_map(cc):
        t = cc.reshape(n_layers, N_SG, SG_GROUPS, c, p).transpose(0, 1, 2, 4, 3)
        return jnp.einsum("ab,lsapc->lsapbc", eye, t).reshape(n_layers, N_SG, SG_STATE, SG_IN)

    b_map = jnp.concatenate([in_map(bb_re), in_map(bb_im)], axis=3).astype(BF16)
    c_map = jnp.concatenate([out_map(c_re), out_map(-c_im)], axis=2).astype(BF16)
    lam_bar = jnp.stack([lb_re, lb_im], axis=1).reshape(n_layers, 2, N_SG, SUBLANES, LANES).transpose(0, 2, 1, 3, 4)
    return b_map, c_map, lam_bar


def _ssm_kernel(u_ref, cgate_ref, bmap_ref, cmap_ref, lam_ref, d_ref, gw_ref, gb_ref, gn_ref, o_ref,
                ybuf, carry_ref, *state_refs, t_rows):
    @pl.when(pl.program_id(1) == 0)
    def _():
        carry_ref[...] = jnp.zeros(carry_ref.shape, F32)

    chunks = SG_STATE // LANES
    for sg in range(N_SG):
        u_sg = u_ref[:, sg * SG_IN:(sg + 1) * SG_IN]
        bu = jnp.dot(u_sg, bmap_ref[sg], preferred_element_type=F32)
        for part in range(2):
            for j in range(chunks):
                col = part * SG_STATE + j * LANES
                state_refs[2 * sg + part][pl.ds(j, t_rows, stride=SUBLANES), :] = bu[:, col:col + LANES]

    lam = [(lam_ref[sg, 0], lam_ref[sg, 1]) for sg in range(N_SG)]

    def step(t, carry):
        rows = pl.ds(pl.multiple_of(t * SUBLANES, SUBLANES), SUBLANES)
        out = []
        for sg in range(N_SG):
            hr, hi = carry[2 * sg], carry[2 * sg + 1]
            ar, ai = lam[sg]
            nr = ar * hr - ai * hi + state_refs[2 * sg][rows, :]
            ni = ar * hi + ai * hr + state_refs[2 * sg + 1][rows, :]
            state_refs[2 * sg][rows, :] = nr
            state_refs[2 * sg + 1][rows, :] = ni
            out += [nr, ni]
        return tuple(out)

    carry = tuple(carry_ref[k] for k in range(2 * N_SG))
    carry = lax.fori_loop(0, t_rows, step, carry, unroll=8)
    for k in range(2 * N_SG):
        carry_ref[k] = carry[k]

    for sg in range(N_SG):
        h = jnp.concatenate(
            [state_refs[2 * sg + part][pl.ds(j, t_rows, stride=SUBLANES), :].astype(BF16)
             for part in range(2) for j in range(chunks)], axis=1)
        ybuf[:, sg * SG_IN:(sg + 1) * SG_IN] = jnp.dot(h, cmap_ref[sg], preferred_element_type=F32)

    y = ybuf[...] + d_ref[...] * u_ref[...].astype(F32)
    y = 0.5 * y * (1.0 + jnp.tanh(math.sqrt(2.0 / math.pi) * (y + 0.044715 * (y * y * y))))
    gate = jnp.dot(y.astype(BF16), gw_ref[...], preferred_element_type=F32) + gb_ref[...]
    gated = y * jax.nn.sigmoid(gate) * _silu(cgate_ref[...].astype(F32))
    o_ref[...] = _rms(gated, gn_ref[...]).astype(o_ref.dtype)


def s5_ssm(proj, b_map, c_map, lam_bar, layer, d_skip, glu_w, glu_b, g_branch, *, batch, seq, t_rows=256):
    m = batch * seq
    nt = seq // t_rows
    const = lambda *shape: pl.BlockSpec(shape, lambda b, t: (0,) * len(shape))
    of_layer = lambda *shape: pl.BlockSpec((None,) + shape, lambda b, t: (layer,) + (0,) * len(shape))
    pcol = lambda c: pl.BlockSpec((t_rows, SSM_WIDTH), lambda b, t: (b * nt + t, c // SSM_WIDTH))
    return pl.pallas_call(
        functools.partial(_ssm_kernel, t_rows=t_rows),
        grid=(batch, nt),
        in_specs=[pcol(COL_C_IN), pcol(COL_C_GATE),
                  of_layer(*b_map.shape[1:]), of_layer(*c_map.shape[1:]), of_layer(*lam_bar.shape[1:]),
                  const(1, SSM_WIDTH), const(SSM_WIDTH, SSM_WIDTH), const(1, SSM_WIDTH), const(1, SSM_WIDTH)],
        out_specs=pl.BlockSpec((t_rows, SSM_WIDTH), lambda b, t: (b * nt + t, 0)),
        out_shape=jax.ShapeDtypeStruct((m, SSM_WIDTH), BF16),
        scratch_shapes=[pltpu.VMEM((t_rows, SSM_WIDTH), F32),
                        pltpu.VMEM((2 * N_SG, SUBLANES, LANES), F32)]
        + [pltpu.VMEM((t_rows * SUBLANES, LANES), F32)] * (2 * N_SG),
        compiler_params=_params("parallel", "arbitrary"),
        name="s5_ssm",
    )(proj, proj, b_map, c_map, lam_bar, d_skip.reshape(1, -1), glu_w, glu_b.reshape(1, -1),
      g_branch.reshape(1, -1))


def _sb_gate_norm_kernel(yb_ref, g0_ref, g1_ref, gn_ref, o_ref):
    gate = jnp.concatenate([g0_ref[...], g1_ref[...]], axis=1).astype(F32)
    o_ref[...] = _rms(yb_ref[...].astype(F32) * _silu(gate), gn_ref[...]).astype(o_ref.dtype)


def sb_gate_norm(yb, proj, g_branch, *, tm=512):
    m = yb.shape[0]
    half = SB_WIDTH // 2
    pcol = lambda c: pl.BlockSpec((tm, half), lambda i: (i, c // half))
    return pl.pallas_call(
        _sb_gate_norm_kernel,
        grid=(m // tm,),
        in_specs=[pl.BlockSpec((tm, SB_WIDTH), lambda i: (i, 0)), pcol(COL_B_GATE), pcol(COL_B_GATE + half),
                  pl.BlockSpec((1, SB_WIDTH), lambda i: (0, 0))],
        out_specs=pl.BlockSpec((tm, SB_WIDTH), lambda i: (i, 0)),
        out_shape=jax.ShapeDtypeStruct((m, SB_WIDTH), BF16),
        compiler_params=_params("parallel"),
        name="sb_gate_norm",
    )(yb, proj, proj, g_branch.reshape(1, -1))


def _xattn_kernel(q_ref, k_ref, v_ref, o_ref, *, n_sub):
    rows = q_ref.shape[0] // n_sub
    subs = [slice(r * rows, (r + 1) * rows) for r in range(n_sub)]
    ss = [lax.dot_general(q_ref[sl, :], k_ref[...], (((1,), (1,)), ((), ())),
                          preferred_element_type=F32) / math.sqrt(XA_HEAD_DIM) for sl in subs]
    es = [jnp.exp(s - jnp.max(s, axis=-1, keepdims=True)) for s in ss]
    ps = [e / jnp.sum(e, axis=-1, keepdims=True) for e in es]
    outs = [jnp.dot(p.astype(BF16), v_ref[...], preferred_element_type=F32) for p in ps]
    for sl, out in zip(subs, outs):
        o_ref[sl, :] = out.astype(o_ref.dtype)


def memory_attention(q, k, v, *, batch, seq, mem_len, tq=2048, n_sub=4):
    m = batch * seq
    nq = seq // tq
    return pl.pallas_call(
        functools.partial(_xattn_kernel, n_sub=n_sub),
        grid=(batch, XA_HEADS, nq),
        in_specs=[pl.BlockSpec((tq, XA_HEAD_DIM), lambda b, h, i: (b * nq + i, h)),
                  pl.BlockSpec((mem_len, XA_HEAD_DIM), lambda b, h, i: (b, h)),
                  pl.BlockSpec((mem_len, XA_HEAD_DIM), lambda b, h, i: (b, h))],
        out_specs=pl.BlockSpec((tq, XA_HEAD_DIM), lambda b, h, i: (b * nq + i, h)),
        out_shape=jax.ShapeDtypeStruct((m, D_MODEL), BF16),
        compiler_params=_params("parallel", "parallel", "arbitrary"),
        name="memory_attention",
    )(q, k, v)


def kernel(x, mem, pre_norm_g, w_in, conv_w, conv_b, conv_ln_g, conv_ln_b, ssm_lambda_re, ssm_lambda_im, ssm_log_dt, ssm_b_re, ssm_b_im, ssm_c_re, ssm_c_im, ssm_d, ssm_glu_w, ssm_glu_b, branch_norm_g, w_out, post_norm_g, xa_pre_g, xa_mem_g, xa_wq, xa_wk, xa_wv, xa_wo, xa_post_g):
    batch, seq, d = x.shape
    mem_len = mem.shape[1]
    depth = w_in.shape[0]
    xs = x.reshape(batch * seq, d)
    mems = mem.reshape(batch * mem_len, d)

    b_map, c_map, lam_bar = ssm_params(ssm_lambda_re, ssm_lambda_im, ssm_log_dt,
                                       ssm_b_re, ssm_b_im, ssm_c_re, ssm_c_im)
    glu_w = ssm_glu_w.astype(BF16)
    ga, gb, gc = (branch_norm_g[:, :CONV_WIDTH], branch_norm_g[:, CONV_WIDTH:CONV_WIDTH + SB_WIDTH],
                  branch_norm_g[:, CONV_WIDTH + SB_WIDTH:])

    h = rmsnorm_cast(xs, pre_norm_g[0])
    for l in range(depth):
        proj = matmul([h], w_in, l, tm=MM_ROWS, tn=MM_COLS, name="in_proj")
        ya = conformer_conv(proj, conv_w[l], conv_b[l], conv_ln_g[l], conv_ln_b[l], ga[l], batch=batch, seq=seq)
        yb = sb_gate_norm(stick_breaking_attention(proj, batch=batch, seq=seq), proj, gb[l])
        yc = s5_ssm(proj, b_map, c_map, lam_bar, l, ssm_d[l], glu_w[l], ssm_glu_b[l], gc[l],
                    batch=batch, seq=seq)
        o = matmul([ya, yb, yc], w_out, l, tm=MM_ROWS, tn=MM_COLS, name="out_proj")
        xs, h2 = residual_norm(o, xs, post_norm_g[l], xa_pre_g[l])

        mn = rmsnorm_cast(mems, xa_mem_g[l])
        q = matmul([h2], xa_wq, l, tm=MM_ROWS, tn=MM_COLS, name="xa_q")
        k = matmul([mn], xa_wk, l, tm=MM_ROWS, tn=MM_COLS, name="xa_k")
        v = matmul([mn], xa_wv, l, tm=MM_ROWS, tn=MM_COLS, name="xa_v")
        a = memory_attention(q, k, v, batch=batch, seq=seq, mem_len=mem_len)
        o = matmul([a], xa_wo, l, tm=MM_ROWS, tn=MM_COLS, name="xa_o")
        g_next = pre_norm_g[l + 1] if l + 1 < depth else None
        xs, h = residual_norm(o, xs, xa_post_g[l], g_next)
    return xs.reshape(batch, seq, d)
```

```python
import functools
import math

import jax
import jax.numpy as jnp
from jax import lax
from jax.experimental import pallas as pl
from jax.experimental.pallas import tpu as pltpu

F32 = jnp.float32
BF16 = jnp.bfloat16

D_MODEL = 4096
CONV_WIDTH = 1024
SB_WIDTH = 2048
SSM_WIDTH = 1024
CONV_TAPS = 31
SB_HEAD_DIM = 128
SB_HEADS = SB_WIDTH // SB_HEAD_DIM
SB_BLOCK = 128
SSM_GROUP = 16
SSM_GROUPS = SSM_WIDTH // SSM_GROUP
SSM_STATE = 64
XA_HEADS = 4
XA_HEAD_DIM = D_MODEL // XA_HEADS
EPS = 1e-6

COL_A_VAL = 0
COL_A_GLU = CONV_WIDTH
COL_A_GATE = 2 * CONV_WIDTH
COL_Q = 3 * CONV_WIDTH
COL_K = COL_Q + SB_WIDTH
COL_V = COL_K + SB_WIDTH
COL_B_GATE = COL_V + SB_WIDTH
COL_C_IN = COL_B_GATE + SB_WIDTH
COL_C_GATE = COL_C_IN + SSM_WIDTH
IN_WIDTH = COL_C_GATE + SSM_WIDTH

LANES = 128
SUBLANES = 8
VMEM_LIMIT_BYTES = 48 * 1024 * 1024

SG_GROUPS = 16
N_SG = SSM_GROUPS // SG_GROUPS
SG_IN = SG_GROUPS * SSM_GROUP
SG_STATE = SG_GROUPS * SSM_STATE
HALO = 32
MM_ROWS, MM_COLS = 2048, 512
MM_VMEM_LIMIT_BYTES = 56 * 1024 * 1024


def _params(*sem, vmem_limit_bytes=VMEM_LIMIT_BYTES):
    return pltpu.CompilerParams(dimension_semantics=sem, vmem_limit_bytes=vmem_limit_bytes)


def _rms(x, g):
    return x * lax.rsqrt(jnp.mean(x * x, axis=-1, keepdims=True) + EPS) * g


def _silu(x):
    return x * jax.nn.sigmoid(x)


def _rmsnorm_cast_kernel(x_ref, g_ref, o_ref):
    o_ref[...] = _rms(x_ref[...], g_ref[...]).astype(o_ref.dtype)


def rmsnorm_cast(x, g, *, tm=256):
    m, d = x.shape
    return pl.pallas_call(
        _rmsnorm_cast_kernel,
        grid=(m // tm,),
        in_specs=[pl.BlockSpec((tm, d), lambda i: (i, 0)),
                  pl.BlockSpec((1, d), lambda i: (0, 0))],
        out_specs=pl.BlockSpec((tm, d), lambda i: (i, 0)),
        out_shape=jax.ShapeDtypeStruct((m, d), BF16),
        compiler_params=_params("parallel"),
        name="rmsnorm_cast",
    )(x, g.reshape(1, d))


def _residual_norm_kernel(o_ref, x_ref, gp_ref, gn_ref, xo_ref, ho_ref):
    x1 = x_ref[...] + _rms(o_ref[...].astype(F32), gp_ref[...])
    xo_ref[...] = x1
    ho_ref[...] = _rms(x1, gn_ref[...]).astype(ho_ref.dtype)


def _residual_kernel(o_ref, x_ref, gp_ref, xo_ref):
    xo_ref[...] = x_ref[...] + _rms(o_ref[...].astype(F32), gp_ref[...])


def residual_norm(o, x, g_post, g_next, *, tm=256):
    m, d = x.shape
    row = pl.BlockSpec((tm, d), lambda i: (i, 0))
    vec = pl.BlockSpec((1, d), lambda i: (0, 0))
    if g_next is None:
        return pl.pallas_call(
            _residual_kernel, grid=(m // tm,),
            in_specs=[row, row, vec], out_specs=row,
            out_shape=jax.ShapeDtypeStruct((m, d), F32),
            compiler_params=_params("parallel"), name="residual",
        )(o, x, g_post.reshape(1, d)), None
    return pl.pallas_call(
        _residual_norm_kernel, grid=(m // tm,),
        in_specs=[row, row, vec, vec], out_specs=[row, row],
        out_shape=[jax.ShapeDtypeStruct((m, d), F32), jax.ShapeDtypeStruct((m, d), BF16)],
        compiler_params=_params("parallel"), name="residual_norm",
    )(o, x, g_post.reshape(1, d), g_next.reshape(1, d))


def _matmul_kernel(*refs):
    *a_refs, w_ref, o_ref = refs
    a = jnp.concatenate([r[...] for r in a_refs], axis=1) if len(a_refs) > 1 else a_refs[0][...]
    o_ref[...] = jnp.dot(a, w_ref[...].astype(BF16), preferred_element_type=F32).astype(o_ref.dtype)


def matmul(a_parts, w, layer, *, tm, tn, out_dtype=BF16, name="matmul"):
    m = a_parts[0].shape[0]
    _, k, n = w.shape
    assert sum(a.shape[1] for a in a_parts) == k
    tm = min(tm, m)
    return pl.pallas_call(
        _matmul_kernel,
        grid=(m // tm, n // tn),
        in_specs=[pl.BlockSpec((tm, a.shape[1]), lambda i, j: (i, 0), pipeline_mode=pl.Buffered(1))
                  for a in a_parts]
        + [pl.BlockSpec((None, k, tn), lambda i, j: (layer, 0, j))],
        out_specs=pl.BlockSpec((tm, tn), lambda i, j: (i, j)),
        out_shape=jax.ShapeDtypeStruct((m, n), out_dtype),
        compiler_params=_params("parallel", "arbitrary", vmem_limit_bytes=MM_VMEM_LIMIT_BYTES),
        name=name,
    )(*a_parts, w)


def _conv_kernel(val_ref, glu_ref, gate_ref, w_ref, b_ref, g_ref, beta_ref, gn_ref, o_ref, ubuf, sbuf, cbuf,
                 *, t_rows):
    @pl.when(pl.program_id(1) == 0)
    def _():
        ubuf[0:HALO, :] = jnp.zeros((HALO, CONV_WIDTH), F32)

    u = val_ref[...].astype(F32) * jax.nn.sigmoid(glu_ref[...].astype(F32))
    ubuf[HALO:HALO + t_rows, :] = u

    first = HALO - (CONV_TAPS - 1)
    span = t_rows + HALO - SUBLANES

    def lane_chunk(c, carry):
        cs = pl.ds(pl.multiple_of(c * LANES, LANES), LANES)
        for r in range(1, SUBLANES):
            sbuf[r, 0:span, :] = ubuf[pl.ds(r, span), cs]
        acc = jnp.zeros((t_rows, LANES), F32)
        for k in range(CONV_TAPS):
            a, r = divmod(first + k, SUBLANES)
            rows = pl.ds(a * SUBLANES, t_rows)
            shifted = ubuf[rows, cs] if r == 0 else sbuf[r, rows, :]
            acc = acc + w_ref[pl.ds(k, 1), cs] * shifted
        cbuf[:, cs] = acc
        return carry

    lax.fori_loop(0, CONV_WIDTH // LANES, lane_chunk, 0)
    ubuf[0:HALO, :] = ubuf[t_rows:t_rows + HALO, :]

    y = cbuf[...] + b_ref[...]
    mu = jnp.mean(y, axis=-1, keepdims=True)
    yc = y - mu
    yn = yc * lax.rsqrt(jnp.mean(yc * yc, axis=-1, keepdims=True) + EPS) * g_ref[...] + beta_ref[...]
    gated = _silu(yn) * _silu(gate_ref[...].astype(F32))
    o_ref[...] = _rms(gated, gn_ref[...]).astype(o_ref.dtype)


def conformer_conv(proj, conv_w, conv_b, ln_g, ln_b, g_branch, *, batch, seq, t_rows=256):
    m = batch * seq
    nt = seq // t_rows
    w = jnp.zeros((HALO, CONV_WIDTH), F32).at[:CONV_TAPS].set(conv_w)
    vec = pl.BlockSpec((1, CONV_WIDTH), lambda b, t: (0, 0))
    pcol = lambda c: pl.BlockSpec((t_rows, CONV_WIDTH), lambda b, t: (b * nt + t, c // CONV_WIDTH))
    return pl.pallas_call(
        functools.partial(_conv_kernel, t_rows=t_rows),
        grid=(batch, nt),
        in_specs=[pcol(COL_A_VAL), pcol(COL_A_GLU), pcol(COL_A_GATE),
                  pl.BlockSpec((HALO, CONV_WIDTH), lambda b, t: (0, 0)),
                  vec, vec, vec, vec],
        out_specs=pl.BlockSpec((t_rows, CONV_WIDTH), lambda b, t: (b * nt + t, 0)),
        out_shape=jax.ShapeDtypeStruct((m, CONV_WIDTH), BF16),
        scratch_shapes=[pltpu.VMEM((t_rows + HALO, CONV_WIDTH), F32),
                        pltpu.VMEM((SUBLANES, t_rows + HALO - SUBLANES, LANES), F32),
                        pltpu.VMEM((t_rows, CONV_WIDTH), F32)],
        compiler_params=_params("parallel", "arbitrary"),
        name="conformer_conv",
    )(proj, proj, proj, w, conv_b.reshape(1, -1), ln_g.reshape(1, -1), ln_b.reshape(1, -1),
      g_branch.reshape(1, -1))


SB_TILE = 256


def _sb_kernel(q_ref, k_ref, v_ref, o_ref, q2_ref, z_ref, own_ref, w_ref, c_ref, acc_ref, *, n_heads):
    t = SB_TILE
    i = pl.program_id(2)
    log2e = 1.0 / math.log(2.0)
    row = lax.broadcasted_iota(jnp.int32, (t, t), 0)
    col = lax.broadcasted_iota(jnp.int32, (t, t), 1)
    suffix = jnp.where(row > col, 1.0, 0.0).astype(BF16)
    before = col < row

    heads = range(n_heads)
    lanes = [slice(hh * SB_HEAD_DIM, (hh + 1) * SB_HEAD_DIM) for hh in heads]

    def key_rows(j):
        return pl.ds(pl.multiple_of(j * t, t), t)

    def scores(j, hh):
        return lax.dot_general(q2_ref[hh], k_ref[key_rows(j), lanes[hh]], (((1,), (1,)), ((), ())),
                               preferred_element_type=F32)

    def weighted_values(slot, j, hh):
        return jnp.dot(w_ref[slot, hh], v_ref[key_rows(j), lanes[hh]], preferred_element_type=F32)

    def trip(j, slot, diagonal):
        drops = []
        for hh in heads:
            z = z_ref[slot, hh]
            l = jnp.log(1.0 + jnp.exp2(-jnp.abs(z))) * log2e
            sp = jnp.maximum(z, 0.0) + l
            own_ref[hh] = z - sp
            drops.append(jnp.where(before, sp, 0.0) if diagonal else sp)
        stacked = jnp.concatenate([d.astype(BF16) for d in drops], axis=0)
        inner_all = jnp.dot(stacked, suffix, preferred_element_type=F32)
        inner = [inner_all[hh * t:(hh + 1) * t] for hh in heads]
        for hh in heads:
            z_ref[1 - slot, hh] = scores(jnp.maximum(j - 1, 0), hh)
        for hh in heads:
            if diagonal:
                acc_ref[hh] = jnp.zeros((t, SB_HEAD_DIM), F32)
            else:
                acc_ref[hh] = acc_ref[hh] + weighted_values(1 - slot, j + 1, hh)
        for hh in heads:
            later = inner[hh] if diagonal else inner[hh] + jnp.concatenate([c_ref[hh]] * (t // LANES), axis=1)
            w = jnp.exp2(own_ref[hh] - later)
            w_ref[slot, hh] = (jnp.where(before, w, 0.0) if diagonal else w).astype(BF16)
            total = jnp.broadcast_to(jnp.sum(drops[hh], axis=-1, keepdims=True), (t, LANES))
            c_ref[hh] = total if diagonal else c_ref[hh] + total

    for hh in heads:
        q2 = q_ref[:, lanes[hh]].astype(F32) * (log2e / math.sqrt(SB_HEAD_DIM))
        q2_ref[hh] = q2.astype(BF16)
    for hh in heads:
        z_ref[0, hh] = scores(i, hh)
    trip(i, 0, True)

    def kv_pair(p, ca):
        j = i - 1 - 2 * p
        trip(j, 1, False)
        trip(j - 1, 0, False)
        return ca

    lax.fori_loop(0, i // 2, kv_pair, 0)

    @pl.when(i % 2 == 1)
    def _():
        trip(0, 1, False)

    for hh in heads:
        o_ref[:, lanes[hh]] = (acc_ref[hh] + weighted_values(i & 1, 0, hh)).astype(o_ref.dtype)


def stick_breaking_attention(proj, *, batch, seq, heads_per_step=8):
    m = batch * seq
    width = heads_per_step * SB_HEAD_DIM
    nq = seq // SB_TILE
    qc, kc, vc = (c // width for c in (COL_Q, COL_K, COL_V))
    q_spec = lambda c0: pl.BlockSpec((SB_TILE, width), lambda b, h, i: (b * nq + i, c0 + h))
    kv_spec = lambda c0: pl.BlockSpec((seq, width), lambda b, h, i: (b, c0 + h))
    tile = (heads_per_step, SB_TILE, SB_TILE)
    head = (heads_per_step, SB_TILE, SB_HEAD_DIM)
    return pl.pallas_call(
        functools.partial(_sb_kernel, n_heads=heads_per_step),
        grid=(batch, SB_HEADS // heads_per_step, nq),
        in_specs=[q_spec(qc), kv_spec(kc), kv_spec(vc)],
        out_specs=q_spec(0),
        out_shape=jax.ShapeDtypeStruct((m, SB_WIDTH), BF16),
        scratch_shapes=[pltpu.VMEM(head, BF16),
                        pltpu.VMEM((2,) + tile, F32),
                        pltpu.VMEM(tile, F32),
                        pltpu.VMEM((2,) + tile, BF16),
                        pltpu.VMEM((heads_per_step, SB_TILE, LANES), F32),
                        pltpu.VMEM(head, F32)],
        compiler_params=_params("parallel", "parallel", "arbitrary"),
        name="stick_breaking",
    )(proj, proj, proj)


def _cmul(ar, ai, br, bi):
    return ar * br - ai * bi, ar * bi + ai * br


def _ssm_param_kernel(lr_ref, li_ref, ldt_ref, lrr_ref, lir_ref, br_ref, bi_ref,
                      pr_ref, pi_ref, bbr_ref, bbi_ref):
    def zoh(lr, li, dt):
        mag = jnp.exp(lr * dt)
        ang = li * dt
        er, ei = mag * jnp.cos(ang), mag * jnp.sin(ang)
        nr, ni = er - 1.0, ei
        den = lr * lr + li * li
        return er, ei, (nr * lr + ni * li) / den, (ni * lr - nr * li) / den

    dt = jnp.exp(ldt_ref[...])
    er, ei, _, _ = zoh(lr_ref[...], li_ref[...], dt)
    pr_ref[...], pi_ref[...] = er, ei

    _, _, cr, ci = zoh(lrr_ref[...], lir_ref[...], dt)
    bbr_ref[...], bbi_ref[...] = _cmul(cr, ci, br_ref[...], bi_ref[...])


def ssm_params(lam_re, lam_im, log_dt, b_re, b_im, c_re, c_im):
    n_layers = lam_re.shape[0]
    g, p, c = SSM_GROUPS, SSM_STATE, SSM_GROUP
    per_layer = lambda *shape: pl.BlockSpec((None,) + shape, lambda l: (l,) + (0,) * len(shape))
    lb_re, lb_im, bb_re, bb_im = pl.pallas_call(
        _ssm_param_kernel,
        grid=(n_layers,),
        in_specs=[per_layer(g, p), per_layer(g, p), per_layer(g, 1), per_layer(g, p * c), per_layer(g, p * c),
                  per_layer(g, p * c), per_layer(g, p * c)],
        out_specs=[per_layer(g, p), per_layer(g, p), per_layer(g, p * c), per_layer(g, p * c)],
        out_shape=[jax.ShapeDtypeStruct((n_layers, g, p), F32)] * 2
        + [jax.ShapeDtypeStruct((n_layers, g, p * c), F32)] * 2,
        name="ssm_params",
    )(lam_re, lam_im, log_dt.reshape(n_layers, g, 1), jnp.repeat(lam_re, c, axis=2),
      jnp.repeat(lam_im, c, axis=2), b_re.reshape(n_layers, g, p * c), b_im.reshape(n_layers, g, p * c))

    eye = jnp.eye(SG_GROUPS, dtype=F32)

    def in_map(bb):
        t = bb.reshape(n_layers, N_SG, SG_GROUPS, p, c).transpose(0, 1, 2, 4, 3)
        return jnp.einsum("ab,lsacp->lsacbp", eye, t).reshape(n_layers, N_SG, SG_IN, SG_STATE)

    def out_map(cc):
        t = cc.reshape(n_layers, N_SG, SG_GROUPS, c, p).transpose(0, 1, 2, 4, 3)
        return jnp.einsum("ab,lsapc->lsapbc", eye, t).reshape(n_layers, N_SG, SG_STATE, SG_IN)

    b_map = jnp.concatenate([in_map(bb_re), in_map(bb_im)], axis=3).astype(BF16)
    c_map = jnp.concatenate([out_map(c_re), out_map(-c_im)], axis=2).astype(BF16)
    lam_bar = jnp.stack([lb_re, lb_im], axis=1).reshape(n_layers, 2, N_SG, SUBLANES, LANES).transpose(0, 2, 1, 3, 4)
    return b_map, c_map, lam_bar


def _ssm_kernel(u_ref, cgate_ref, bmap_ref, cmap_ref, lam_ref, d_ref, gw_ref, gb_ref, gn_ref, o_ref,
                ybuf, carry_ref, *state_refs, t_rows):
    @pl.when(pl.program_id(1) == 0)
    def _():
        carry_ref[...] = jnp.zeros(carry_ref.shape, F32)

    chunks = SG_STATE // LANES
    for sg in range(N_SG):
        u_sg = u_ref[:, sg * SG_IN:(sg + 1) * SG_IN]
        bu = jnp.dot(u_sg, bmap_ref[sg], preferred_element_type=F32)
        for part in range(2):
            for j in range(chunks):
                col = part * SG_STATE + j * LANES
                state_refs[2 * sg + part][pl.ds(j, t_rows, stride=SUBLANES), :] = bu[:, col:col + LANES]

    lam = [(lam_ref[sg, 0], lam_ref[sg, 1]) for sg in range(N_SG)]

    def step(t, carry):
        rows = pl.ds(pl.multiple_of(t * SUBLANES, SUBLANES), SUBLANES)
        out = []
        for sg in range(N_SG):
            hr, hi = carry[2 * sg], carry[2 * sg + 1]
            ar, ai = lam[sg]
            nr = ar * hr - ai * hi + state_refs[2 * sg][rows, :]
            ni = ar * hi + ai * hr + state_refs[2 * sg + 1][rows, :]
            state_refs[2 * sg][rows, :] = nr
            state_refs[2 * sg + 1][rows, :] = ni
            out += [nr, ni]
        return tuple(out)

    carry = tuple(carry_ref[k] for k in range(2 * N_SG))
    carry = lax.fori_loop(0, t_rows, step, carry, unroll=8)
    for k in range(2 * N_SG):
        carry_ref[k] = carry[k]

    for sg in range(N_SG):
        h = jnp.concatenate(
            [state_refs[2 * sg + part][pl.ds(j, t_rows, stride=SUBLANES), :].astype(BF16)
             for part in range(2) for j in range(chunks)], axis=1)
        ybuf[:, sg * SG_IN:(sg + 1) * SG_IN] = jnp.dot(h, cmap_ref[sg], preferred_element_type=F32)

    y = ybuf[...] + d_ref[...] * u_ref[...].astype(F32)
    y = 0.5 * y * (1.0 + jnp.tanh(math.sqrt(2.0 / math.pi) * (y + 0.044715 * (y * y * y))))
    gate = jnp.dot(y.astype(BF16), gw_ref[...], preferred_element_type=F32) + gb_ref[...]
    gated = y * jax.nn.sigmoid(gate) * _silu(cgate_ref[...].astype(F32))
    o_ref[...] = _rms(gated, gn_ref[...]).astype(o_ref.dtype)


def s5_ssm(proj, b_map, c_map, lam_bar, layer, d_skip, glu_w, glu_b, g_branch, *, batch, seq, t_rows=512):
    m = batch * seq
    nt = seq // t_rows
    const = lambda *shape: pl.BlockSpec(shape, lambda b, t: (0,) * len(shape))
    of_layer = lambda *shape: pl.BlockSpec((None,) + shape, lambda b, t: (layer,) + (0,) * len(shape))
    pcol = lambda c: pl.BlockSpec((t_rows, SSM_WIDTH), lambda b, t: (b * nt + t, c // SSM_WIDTH))
    return pl.pallas_call(
        functools.partial(_ssm_kernel, t_rows=t_rows),
        grid=(batch, nt),
        in_specs=[pcol(COL_C_IN), pcol(COL_C_GATE),
                  of_layer(*b_map.shape[1:]), of_layer(*c_map.shape[1:]), of_layer(*lam_bar.shape[1:]),
                  const(1, SSM_WIDTH), const(SSM_WIDTH, SSM_WIDTH), const(1, SSM_WIDTH), const(1, SSM_WIDTH)],
        out_specs=pl.BlockSpec((t_rows, SSM_WIDTH), lambda b, t: (b * nt + t, 0)),
        out_shape=jax.ShapeDtypeStruct((m, SSM_WIDTH), BF16),
        scratch_shapes=[pltpu.VMEM((t_rows, SSM_WIDTH), F32),
                        pltpu.VMEM((2 * N_SG, SUBLANES, LANES), F32)]
        + [pltpu.VMEM((t_rows * SUBLANES, LANES), F32)] * (2 * N_SG),
        compiler_params=_params("parallel", "arbitrary"),
        name="s5_ssm",
    )(proj, proj, b_map, c_map, lam_bar, d_skip.reshape(1, -1), glu_w, glu_b.reshape(1, -1),
      g_branch.reshape(1, -1))


def _sb_gate_norm_kernel(yb_ref, g0_ref, g1_ref, gn_ref, o_ref):
    gate = jnp.concatenate([g0_ref[...], g1_ref[...]], axis=1).astype(F32)
    o_ref[...] = _rms(yb_ref[...].astype(F32) * _silu(gate), gn_ref[...]).astype(o_ref.dtype)


def sb_gate_norm(yb, proj, g_branch, *, tm=512):
    m = yb.shape[0]
    half = SB_WIDTH // 2
    pcol = lambda c: pl.BlockSpec((tm, half), lambda i: (i, c // half))
    return pl.pallas_call(
        _sb_gate_norm_kernel,
        grid=(m // tm,),
        in_specs=[pl.BlockSpec((tm, SB_WIDTH), lambda i: (i, 0)), pcol(COL_B_GATE), pcol(COL_B_GATE + half),
                  pl.BlockSpec((1, SB_WIDTH), lambda i: (0, 0))],
        out_specs=pl.BlockSpec((tm, SB_WIDTH), lambda i: (i, 0)),
        out_shape=jax.ShapeDtypeStruct((m, SB_WIDTH), BF16),
        compiler_params=_params("parallel"),
        name="sb_gate_norm",
    )(yb, proj, proj, g_branch.reshape(1, -1))


def _xattn_kernel(q_ref, k_ref, v_ref, o_ref, *, n_sub):
    rows = q_ref.shape[0] // n_sub
    subs = [slice(r * rows, (r + 1) * rows) for r in range(n_sub)]
    ss = [lax.dot_general(q_ref[sl, :], k_ref[...], (((1,), (1,)), ((), ())),
                          preferred_element_type=F32) / math.sqrt(XA_HEAD_DIM) for sl in subs]
    es = [jnp.exp(s - jnp.max(s, axis=-1, keepdims=True)) for s in ss]
    ps = [e / jnp.sum(e, axis=-1, keepdims=True) for e in es]
    outs = [jnp.dot(p.astype(BF16), v_ref[...], preferred_element_type=F32) for p in ps]
    for sl, out in zip(subs, outs):
        o_ref[sl, :] = out.astype(o_ref.dtype)


def memory_attention(q, k, v, *, batch, seq, mem_len, tq=2048, n_sub=4):
    m = batch * seq
    nq = seq // tq
    return pl.pallas_call(
        functools.partial(_xattn_kernel, n_sub=n_sub),
        grid=(batch, XA_HEADS, nq),
        in_specs=[pl.BlockSpec((tq, XA_HEAD_DIM), lambda b, h, i: (b * nq + i, h)),
                  pl.BlockSpec((mem_len, XA_HEAD_DIM), lambda b, h, i: (b, h)),
                  pl.BlockSpec((mem_len, XA_HEAD_DIM), lambda b, h, i: (b, h))],
        out_specs=pl.BlockSpec((tq, XA_HEAD_DIM), lambda b, h, i: (b * nq + i, h)),
        out_shape=jax.ShapeDtypeStruct((m, D_MODEL), BF16),
        compiler_params=_params("parallel", "parallel", "arbitrary"),
        name="memory_attention",
    )(q, k, v)


def kernel(x, mem, pre_norm_g, w_in, conv_w, conv_b, conv_ln_g, conv_ln_b, ssm_lambda_re, ssm_lambda_im, ssm_log_dt, ssm_b_re, ssm_b_im, ssm_c_re, ssm_c_im, ssm_d, ssm_glu_w, ssm_glu_b, branch_norm_g, w_out, post_norm_g, xa_pre_g, xa_mem_g, xa_wq, xa_wk, xa_wv, xa_wo, xa_post_g):
    batch, seq, d = x.shape
    mem_len = mem.shape[1]
    depth = w_in.shape[0]
    xs = x.reshape(batch * seq, d)
    mems = mem.reshape(batch * mem_len, d)

    b_map, c_map, lam_bar = ssm_params(ssm_lambda_re, ssm_lambda_im, ssm_log_dt,
                                       ssm_b_re, ssm_b_im, ssm_c_re, ssm_c_im)
    glu_w = ssm_glu_w.astype(BF16)
    ga, gb, gc = (branch_norm_g[:, :CONV_WIDTH], branch_norm_g[:, CONV_WIDTH:CONV_WIDTH + SB_WIDTH],
                  branch_norm_g[:, CONV_WIDTH + SB_WIDTH:])

    h = rmsnorm_cast(xs, pre_norm_g[0])
    for l in range(depth):
        proj = matmul([h], w_in, l, tm=MM_ROWS, tn=MM_COLS, name="in_proj")
        ya = conformer_conv(proj, conv_w[l], conv_b[l], conv_ln_g[l], conv_ln_b[l], ga[l], batch=batch, seq=seq)
        yb = sb_gate_norm(stick_breaking_attention(proj, batch=batch, seq=seq), proj, gb[l])
        yc = s5_ssm(proj, b_map, c_map, lam_bar, l, ssm_d[l], glu_w[l], ssm_glu_b[l], gc[l],
                    batch=batch, seq=seq)
        o = matmul([ya, yb, yc], w_out, l, tm=MM_ROWS, tn=MM_COLS, name="out_proj")
        xs, h2 = residual_norm(o, xs, post_norm_g[l], xa_pre_g[l])

        mn = rmsnorm_cast(mems, xa_mem_g[l])
        q = matmul([h2], xa_wq, l, tm=MM_ROWS, tn=MM_COLS, name="xa_q")
        k = matmul([mn], xa_wk, l, tm=MM_ROWS, tn=MM_COLS, name="xa_k")
        v = matmul([mn], xa_wv, l, tm=MM_ROWS, tn=MM_COLS, name="xa_v")
        a = memory_attention(q, k, v, batch=batch, seq=seq, mem_len=mem_len)
        o = matmul([a], xa_wo, l, tm=MM_ROWS, tn=MM_COLS, name="xa_o")
        g_next = pre_norm_g[l + 1] if l + 1 < depth else None
        xs, h = residual_norm(o, xs, xa_post_g[l], g_next)
    return xs.reshape(batch, seq, d)
```

```python
import functools
import math

import jax
import jax.numpy as jnp
from jax import lax
from jax.experimental import pallas as pl
from jax.experimental.pallas import tpu as pltpu

F32 = jnp.float32
BF16 = jnp.bfloat16

D_MODEL = 4096
CONV_WIDTH = 1024
SB_WIDTH = 2048
SSM_WIDTH = 1024
CONV_TAPS = 31
SB_HEAD_DIM = 128
SB_HEADS = SB_WIDTH // SB_HEAD_DIM
SB_BLOCK = 128
SSM_GROUP = 16
SSM_GROUPS = SSM_WIDTH // SSM_GROUP
SSM_STATE = 64
XA_HEADS = 4
XA_HEAD_DIM = D_MODEL // XA_HEADS
EPS = 1e-6

COL_A_VAL = 0
COL_A_GLU = CONV_WIDTH
COL_A_GATE = 2 * CONV_WIDTH
COL_Q = 3 * CONV_WIDTH
COL_K = COL_Q + SB_WIDTH
COL_V = COL_K + SB_WIDTH
COL_B_GATE = COL_V + SB_WIDTH
COL_C_IN = COL_B_GATE + SB_WIDTH
COL_C_GATE = COL_C_IN + SSM_WIDTH
IN_WIDTH = COL_C_GATE + SSM_WIDTH

LANES = 128
SUBLANES = 8
VMEM_LIMIT_BYTES = 48 * 1024 * 1024

SG_GROUPS = 16
N_SG = SSM_GROUPS // SG_GROUPS
SG_IN = SG_GROUPS * SSM_GROUP
SG_STATE = SG_GROUPS * SSM_STATE
HALO = 32
MM_ROWS, MM_COLS = 2048, 512
MM_VMEM_LIMIT_BYTES = 56 * 1024 * 1024


def _params(*sem, vmem_limit_bytes=VMEM_LIMIT_BYTES):
    return pltpu.CompilerParams(dimension_semantics=sem, vmem_limit_bytes=vmem_limit_bytes)


def _rms(x, g):
    return x * lax.rsqrt(jnp.mean(x * x, axis=-1, keepdims=True) + EPS) * g


def _silu(x):
    return x * jax.nn.sigmoid(x)


def _rmsnorm_cast_kernel(x_ref, g_ref, o_ref):
    o_ref[...] = _rms(x_ref[...], g_ref[...]).astype(o_ref.dtype)


def rmsnorm_cast(x, g, *, tm=256):
    m, d = x.shape
    return pl.pallas_call(
        _rmsnorm_cast_kernel,
        grid=(m // tm,),
        in_specs=[pl.BlockSpec((tm, d), lambda i: (i, 0)),
                  pl.BlockSpec((1, d), lambda i: (0, 0))],
        out_specs=pl.BlockSpec((tm, d), lambda i: (i, 0)),
        out_shape=jax.ShapeDtypeStruct((m, d), BF16),
        compiler_params=_params("parallel"),
        name="rmsnorm_cast",
    )(x, g.reshape(1, d))


def _residual_norm_kernel(o_ref, x_ref, gp_ref, gn_ref, xo_ref, ho_ref):
    x1 = x_ref[...] + _rms(o_ref[...].astype(F32), gp_ref[...])
    xo_ref[...] = x1
    ho_ref[...] = _rms(x1, gn_ref[...]).astype(ho_ref.dtype)


def _residual_kernel(o_ref, x_ref, gp_ref, xo_ref):
    xo_ref[...] = x_ref[...] + _rms(o_ref[...].astype(F32), gp_ref[...])


def residual_norm(o, x, g_post, g_next, *, tm=256):
    m, d = x.shape
    row = pl.BlockSpec((tm, d), lambda i: (i, 0))
    vec = pl.BlockSpec((1, d), lambda i: (0, 0))
    if g_next is None:
        return pl.pallas_call(
            _residual_kernel, grid=(m // tm,),
            in_specs=[row, row, vec], out_specs=row,
            out_shape=jax.ShapeDtypeStruct((m, d), F32),
            compiler_params=_params("parallel"), name="residual",
        )(o, x, g_post.reshape(1, d)), None
    return pl.pallas_call(
        _residual_norm_kernel, grid=(m // tm,),
        in_specs=[row, row, vec, vec], out_specs=[row, row],
        out_shape=[jax.ShapeDtypeStruct((m, d), F32), jax.ShapeDtypeStruct((m, d), BF16)],
        compiler_params=_params("parallel"), name="residual_norm",
    )(o, x, g_post.reshape(1, d), g_next.reshape(1, d))


def _matmul_kernel(*refs):
    *a_refs, w_ref, o_ref = refs
    a = jnp.concatenate([r[...] for r in a_refs], axis=1) if len(a_refs) > 1 else a_refs[0][...]
    o_ref[...] = jnp.dot(a, w_ref[...].astype(BF16), preferred_element_type=F32).astype(o_ref.dtype)


def matmul(a_parts, w, layer, *, tm, tn, out_dtype=BF16, name="matmul"):
    m = a_parts[0].shape[0]
    _, k, n = w.shape
    assert sum(a.shape[1] for a in a_parts) == k
    tm = min(tm, m)
    return pl.pallas_call(
        _matmul_kernel,
        grid=(m // tm, n // tn),
        in_specs=[pl.BlockSpec((tm, a.shape[1]), lambda i, j: (i, 0), pipeline_mode=pl.Buffered(1))
                  for a in a_parts]
        + [pl.BlockSpec((None, k, tn), lambda i, j: (layer, 0, j))],
        out_specs=pl.BlockSpec((tm, tn), lambda i, j: (i, j)),
        out_shape=jax.ShapeDtypeStruct((m, n), out_dtype),
        compiler_params=_params("parallel", "arbitrary", vmem_limit_bytes=MM_VMEM_LIMIT_BYTES),
        name=name,
    )(*a_parts, w)


def _conv_kernel(val_ref, glu_ref, gate_ref, w_ref, b_ref, g_ref, beta_ref, gn_ref, o_ref, ubuf, sbuf, cbuf,
                 *, t_rows):
    @pl.when(pl.program_id(1) == 0)
    def _():
        ubuf[0:HALO, :] = jnp.zeros((HALO, CONV_WIDTH), F32)

    u = val_ref[...].astype(F32) * jax.nn.sigmoid(glu_ref[...].astype(F32))
    ubuf[HALO:HALO + t_rows, :] = u

    first = HALO - (CONV_TAPS - 1)
    span = t_rows + HALO - SUBLANES

    def lane_chunk(c, carry):
        cs = pl.ds(pl.multiple_of(c * LANES, LANES), LANES)
        for r in range(1, SUBLANES):
            sbuf[r, 0:span, :] = ubuf[pl.ds(r, span), cs]
        acc = jnp.zeros((t_rows, LANES), F32)
        for k in range(CONV_TAPS):
            a, r = divmod(first + k, SUBLANES)
            rows = pl.ds(a * SUBLANES, t_rows)
            shifted = ubuf[rows, cs] if r == 0 else sbuf[r, rows, :]
            acc = acc + w_ref[pl.ds(k, 1), cs] * shifted
        cbuf[:, cs] = acc
        return carry

    lax.fori_loop(0, CONV_WIDTH // LANES, lane_chunk, 0)
    ubuf[0:HALO, :] = ubuf[t_rows:t_rows + HALO, :]

    y = cbuf[...] + b_ref[...]
    mu = jnp.mean(y, axis=-1, keepdims=True)
    yc = y - mu
    yn = yc * lax.rsqrt(jnp.mean(yc * yc, axis=-1, keepdims=True) + EPS) * g_ref[...] + beta_ref[...]
    gated = _silu(yn) * _silu(gate_ref[...].astype(F32))
    o_ref[...] = _rms(gated, gn_ref[...]).astype(o_ref.dtype)


def conformer_conv(proj, conv_w, conv_b, ln_g, ln_b, g_branch, *, batch, seq, t_rows=256):
    m = batch * seq
    nt = seq // t_rows
    w = jnp.zeros((HALO, CONV_WIDTH), F32).at[:CONV_TAPS].set(conv_w)
    vec = pl.BlockSpec((1, CONV_WIDTH), lambda b, t: (0, 0))
    pcol = lambda c: pl.BlockSpec((t_rows, CONV_WIDTH), lambda b, t: (b * nt + t, c // CONV_WIDTH))
    return pl.pallas_call(
        functools.partial(_conv_kernel, t_rows=t_rows),
        grid=(batch, nt),
        in_specs=[pcol(COL_A_VAL), pcol(COL_A_GLU), pcol(COL_A_GATE),
                  pl.BlockSpec((HALO, CONV_WIDTH), lambda b, t: (0, 0)),
                  vec, vec, vec, vec],
        out_specs=pl.BlockSpec((t_rows, CONV_WIDTH), lambda b, t: (b * nt + t, 0)),
        out_shape=jax.ShapeDtypeStruct((m, CONV_WIDTH), BF16),
        scratch_shapes=[pltpu.VMEM((t_rows + HALO, CONV_WIDTH), F32),
                        pltpu.VMEM((SUBLANES, t_rows + HALO - SUBLANES, LANES), F32),
                        pltpu.VMEM((t_rows, CONV_WIDTH), F32)],
        compiler_params=_params("parallel", "arbitrary"),
        name="conformer_conv",
    )(proj, proj, proj, w, conv_b.reshape(1, -1), ln_g.reshape(1, -1), ln_b.reshape(1, -1),
      g_branch.reshape(1, -1))


SB_TILE = 256


def _sb_kernel(q_ref, k_ref, v_ref, o_ref, q2_ref, z_ref, own_ref, w_ref, c_ref, acc_ref, *, n_heads):
    t = SB_TILE
    i = pl.program_id(2)
    log2e = 1.0 / math.log(2.0)
    row = lax.broadcasted_iota(jnp.int32, (t, t), 0)
    col = lax.broadcasted_iota(jnp.int32, (t, t), 1)
    suffix = jnp.where(row > col, 1.0, 0.0).astype(BF16)
    before = col < row

    heads = range(n_heads)
    lanes = [slice(hh * SB_HEAD_DIM, (hh + 1) * SB_HEAD_DIM) for hh in heads]

    def key_rows(j):
        return pl.ds(pl.multiple_of(j * t, t), t)

    def scores(j, hh):
        return lax.dot_general(q2_ref[hh], k_ref[key_rows(j), lanes[hh]], (((1,), (1,)), ((), ())),
                               preferred_element_type=F32)

    def weighted_values(slot, j, hh):
        return jnp.dot(w_ref[slot, hh], v_ref[key_rows(j), lanes[hh]], preferred_element_type=F32)

    def trip(j, slot, diagonal):
        drops = []
        for hh in heads:
            z = z_ref[slot, hh]
            l = jnp.log(1.0 + jnp.exp2(-jnp.abs(z))) * log2e
            sp = jnp.maximum(z, 0.0) + l
            own_ref[hh] = z - sp
            drops.append(jnp.where(before, sp, 0.0) if diagonal else sp)
        stacked = jnp.concatenate([d.astype(BF16) for d in drops], axis=0)
        inner_all = jnp.dot(stacked, suffix, preferred_element_type=F32)
        inner = [inner_all[hh * t:(hh + 1) * t] for hh in heads]
        for hh in heads:
            z_ref[1 - slot, hh] = scores(jnp.maximum(j - 1, 0), hh)
        for hh in heads:
            if diagonal:
                acc_ref[hh] = jnp.zeros((t, SB_HEAD_DIM), F32)
            else:
                acc_ref[hh] = acc_ref[hh] + weighted_values(1 - slot, j + 1, hh)
        for hh in heads:
            later = inner[hh] if diagonal else inner[hh] + jnp.concatenate([c_ref[hh]] * (t // LANES), axis=1)
            w = jnp.exp2(own_ref[hh] - later)
            w_ref[slot, hh] = (jnp.where(before, w, 0.0) if diagonal else w).astype(BF16)
            total = jnp.broadcast_to(jnp.sum(drops[hh], axis=-1, keepdims=True), (t, LANES))
            c_ref[hh] = total if diagonal else c_ref[hh] + total

    for hh in heads:
        q2 = q_ref[:, lanes[hh]].astype(F32) * (log2e / math.sqrt(SB_HEAD_DIM))
        q2_ref[hh] = q2.astype(BF16)
    for hh in heads:
        z_ref[0, hh] = scores(i, hh)
    trip(i, 0, True)

    def kv_pair(p, ca):
        j = i - 1 - 2 * p
        trip(j, 1, False)
        trip(j - 1, 0, False)
        return ca

    lax.fori_loop(0, i // 2, kv_pair, 0)

    @pl.when(i % 2 == 1)
    def _():
        trip(0, 1, False)

    for hh in heads:
        o_ref[:, lanes[hh]] = (acc_ref[hh] + weighted_values(i & 1, 0, hh)).astype(o_ref.dtype)


def stick_breaking_attention(proj, *, batch, seq, heads_per_step=8):
    m = batch * seq
    width = heads_per_step * SB_HEAD_DIM
    nq = seq // SB_TILE
    qc, kc, vc = (c // width for c in (COL_Q, COL_K, COL_V))
    q_spec = lambda c0: pl.BlockSpec((SB_TILE, width), lambda b, h, i: (b * nq + i, c0 + h))
    kv_spec = lambda c0: pl.BlockSpec((seq, width), lambda b, h, i: (b, c0 + h))
    tile = (heads_per_step, SB_TILE, SB_TILE)
    head = (heads_per_step, SB_TILE, SB_HEAD_DIM)
    return pl.pallas_call(
        functools.partial(_sb_kernel, n_heads=heads_per_step),
        grid=(batch, SB_HEADS // heads_per_step, nq),
        in_specs=[q_spec(qc), kv_spec(kc), kv_spec(vc)],
        out_specs=q_spec(0),
        out_shape=jax.ShapeDtypeStruct((m, SB_WIDTH), BF16),
        scratch_shapes=[pltpu.VMEM(head, BF16),
                        pltpu.VMEM((2,) + tile, F32),
                        pltpu.VMEM(tile, F32),
                        pltpu.VMEM((2,) + tile, BF16),
                        pltpu.VMEM((heads_per_step, SB_TILE, LANES), F32),
                        pltpu.VMEM(head, F32)],
        compiler_params=_params("parallel", "parallel", "arbitrary"),
        name="stick_breaking",
    )(proj, proj, proj)


def _cmul(ar, ai, br, bi):
    return ar * br - ai * bi, ar * bi + ai * br


def _ssm_param_kernel(lr_ref, li_ref, ldt_ref, lrr_ref, lir_ref, br_ref, bi_ref,
                      pr_ref, pi_ref, bbr_ref, bbi_ref):
    def zoh(lr, li, dt):
        mag = jnp.exp(lr * dt)
        ang = li * dt
        er, ei = mag * jnp.cos(ang), mag * jnp.sin(ang)
        nr, ni = er - 1.0, ei
        den = lr * lr + li * li
        return er, ei, (nr * lr + ni * li) / den, (ni * lr - nr * li) / den

    dt = jnp.exp(ldt_ref[...])
    er, ei, _, _ = zoh(lr_ref[...], li_ref[...], dt)
    pr_ref[...], pi_ref[...] = er, ei

    _, _, cr, ci = zoh(lrr_ref[...], lir_ref[...], dt)
    bbr_ref[...], bbi_ref[...] = _cmul(cr, ci, br_ref[...], bi_ref[...])


def ssm_params(lam_re, lam_im, log_dt, b_re, b_im, c_re, c_im):
    n_layers = lam_re.shape[0]
    g, p, c = SSM_GROUPS, SSM_STATE, SSM_GROUP
    per_layer = lambda *shape: pl.BlockSpec((None,) + shape, lambda l: (l,) + (0,) * len(shape))
    lb_re, lb_im, bb_re, bb_im = pl.pallas_call(
        _ssm_param_kernel,
        grid=(n_layers,),
        in_specs=[per_layer(g, p), per_layer(g, p), per_layer(g, 1), per_layer(g, p * c), per_layer(g, p * c),
                  per_layer(g, p * c), per_layer(g, p * c)],
        out_specs=[per_layer(g, p), per_layer(g, p), per_layer(g, p * c), per_layer(g, p * c)],
        out_shape=[jax.ShapeDtypeStruct((n_layers, g, p), F32)] * 2
        + [jax.ShapeDtypeStruct((n_layers, g, p * c), F32)] * 2,
        name="ssm_params",
    )(lam_re, lam_im, log_dt.reshape(n_layers, g, 1), jnp.repeat(lam_re, c, axis=2),
      jnp.repeat(lam_im, c, axis=2), b_re.reshape(n_layers, g, p * c), b_im.reshape(n_layers, g, p * c))

    def block_diagonal(t, rows_per_group, cols_per_group):
        tiled = jnp.tile(t, (1, 1, 1, SG_GROUPS))
        row_group = jnp.arange(tiled.shape[2])[:, None] // rows_per_group
        col_group = jnp.arange(tiled.shape[3])[None, :] // cols_per_group
        return jnp.where(row_group == col_group, tiled, 0.0)

    def in_map(bb):
        t = bb.reshape(n_layers, N_SG, SG_GROUPS, p, c).transpose(0, 1, 2, 4, 3)
        return block_diagonal(t.reshape(n_layers, N_SG, SG_IN, p), c, p)

    def out_map(cc):
        t = cc.reshape(n_layers, N_SG, SG_GROUPS, c, p).transpose(0, 1, 2, 4, 3)
        return block_diagonal(t.reshape(n_layers, N_SG, SG_STATE, c), p, c)

    b_map = jnp.concatenate([in_map(bb_re), in_map(bb_im)], axis=3).astype(BF16)
    c_map = jnp.concatenate([out_map(c_re), out_map(-c_im)], axis=2).astype(BF16)
    lam_bar = jnp.stack([lb_re, lb_im], axis=1).reshape(n_layers, 2, N_SG, SUBLANES, LANES).transpose(0, 2, 1, 3, 4)
    return b_map, c_map, lam_bar


def _ssm_kernel(u_ref, cgate_ref, bmap_ref, cmap_ref, lam_ref, d_ref, gw_ref, gb_ref, gn_ref, o_ref,
                ybuf, carry_ref, *state_refs, t_rows):
    @pl.when(pl.program_id(1) == 0)
    def _():
        carry_ref[...] = jnp.zeros(carry_ref.shape, F32)

    chunks = SG_STATE // LANES
    for sg in range(N_SG):
        u_sg = u_ref[:, sg * SG_IN:(sg + 1) * SG_IN]
        bu = jnp.dot(u_sg, bmap_ref[sg], preferred_element_type=F32)
        for part in range(2):
            for j in range(chunks):
                col = part * SG_STATE + j * LANES
                state_refs[2 * sg + part][pl.ds(j, t_rows, stride=SUBLANES), :] = bu[:, col:col + LANES]

    lam = [(lam_ref[sg, 0], lam_ref[sg, 1]) for sg in range(N_SG)]

    def step(t, carry):
        rows = pl.ds(pl.multiple_of(t * SUBLANES, SUBLANES), SUBLANES)
        out = []
        for sg in range(N_SG):
            hr, hi = carry[2 * sg], carry[2 * sg + 1]
            ar, ai = lam[sg]
            nr = ar * hr - ai * hi + state_refs[2 * sg][rows, :]
            ni = ar * hi + ai * hr + state_refs[2 * sg + 1][rows, :]
            state_refs[2 * sg][rows, :] = nr
            state_refs[2 * sg + 1][rows, :] = ni
            out += [nr, ni]
        return tuple(out)

    carry = tuple(carry_ref[k] for k in range(2 * N_SG))
    carry = lax.fori_loop(0, t_rows, step, carry, unroll=8)
    for k in range(2 * N_SG):
        carry_ref[k] = carry[k]

    for sg in range(N_SG):
        h = jnp.concatenate(
            [state_refs[2 * sg + part][pl.ds(j, t_rows, stride=SUBLANES), :].astype(BF16)
             for part in range(2) for j in range(chunks)], axis=1)
        ybuf[:, sg * SG_IN:(sg + 1) * SG_IN] = jnp.dot(h, cmap_ref[sg], preferred_element_type=F32)

    y = ybuf[...] + d_ref[...] * u_ref[...].astype(F32)
    y = 0.5 * y * (1.0 + jnp.tanh(math.sqrt(2.0 / math.pi) * (y + 0.044715 * (y * y * y))))
    gate = jnp.dot(y.astype(BF16), gw_ref[...], preferred_element_type=F32) + gb_ref[...]
    gated = y * jax.nn.sigmoid(gate) * _silu(cgate_ref[...].astype(F32))
    o_ref[...] = _rms(gated, gn_ref[...]).astype(o_ref.dtype)


def s5_ssm(proj, b_map, c_map, lam_bar, layer, d_skip, glu_w, glu_b, g_branch, *, batch, seq, t_rows=512):
    m = batch * seq
    nt = seq // t_rows
    const = lambda *shape: pl.BlockSpec(shape, lambda b, t: (0,) * len(shape))
    of_layer = lambda *shape: pl.BlockSpec((None,) + shape, lambda b, t: (layer,) + (0,) * len(shape))
    pcol = lambda c: pl.BlockSpec((t_rows, SSM_WIDTH), lambda b, t: (b * nt + t, c // SSM_WIDTH))
    return pl.pallas_call(
        functools.partial(_ssm_kernel, t_rows=t_rows),
        grid=(batch, nt),
        in_specs=[pcol(COL_C_IN), pcol(COL_C_GATE),
                  of_layer(*b_map.shape[1:]), of_layer(*c_map.shape[1:]), of_layer(*lam_bar.shape[1:]),
                  const(1, SSM_WIDTH), const(SSM_WIDTH, SSM_WIDTH), const(1, SSM_WIDTH), const(1, SSM_WIDTH)],
        out_specs=pl.BlockSpec((t_rows, SSM_WIDTH), lambda b, t: (b * nt + t, 0)),
        out_shape=jax.ShapeDtypeStruct((m, SSM_WIDTH), BF16),
        scratch_shapes=[pltpu.VMEM((t_rows, SSM_WIDTH), F32),
                        pltpu.VMEM((2 * N_SG, SUBLANES, LANES), F32)]
        + [pltpu.VMEM((t_rows * SUBLANES, LANES), F32)] * (2 * N_SG),
        compiler_params=_params("parallel", "arbitrary"),
        name="s5_ssm",
    )(proj, proj, b_map, c_map, lam_bar, d_skip.reshape(1, -1), glu_w, glu_b.reshape(1, -1),
      g_branch.reshape(1, -1))


def _sb_gate_norm_kernel(yb_ref, g0_ref, g1_ref, gn_ref, o_ref):
    gate = jnp.concatenate([g0_ref[...], g1_ref[...]], axis=1).astype(F32)
    o_ref[...] = _rms(yb_ref[...].astype(F32) * _silu(gate), gn_ref[...]).astype(o_ref.dtype)


def sb_gate_norm(yb, proj, g_branch, *, tm=512):
    m = yb.shape[0]
    half = SB_WIDTH // 2
    pcol = lambda c: pl.BlockSpec((tm, half), lambda i: (i, c // half))
    return pl.pallas_call(
        _sb_gate_norm_kernel,
        grid=(m // tm,),
        in_specs=[pl.BlockSpec((tm, SB_WIDTH), lambda i: (i, 0)), pcol(COL_B_GATE), pcol(COL_B_GATE + half),
                  pl.BlockSpec((1, SB_WIDTH), lambda i: (0, 0))],
        out_specs=pl.BlockSpec((tm, SB_WIDTH), lambda i: (i, 0)),
        out_shape=jax.ShapeDtypeStruct((m, SB_WIDTH), BF16),
        compiler_params=_params("parallel"),
        name="sb_gate_norm",
    )(yb, proj, proj, g_branch.reshape(1, -1))


def _xattn_kernel(q_ref, k_ref, v_ref, o_ref, *, n_sub):
    rows = q_ref.shape[0] // n_sub
    subs = [slice(r * rows, (r + 1) * rows) for r in range(n_sub)]
    ss = [lax.dot_general(q_ref[sl, :], k_ref[...], (((1,), (1,)), ((), ())),
                          preferred_element_type=F32) / math.sqrt(XA_HEAD_DIM) for sl in subs]
    es = [jnp.exp(s - jnp.max(s, axis=-1, keepdims=True)) for s in ss]
    ps = [e / jnp.sum(e, axis=-1, keepdims=True) for e in es]
    outs = [jnp.dot(p.astype(BF16), v_ref[...], preferred_element_type=F32) for p in ps]
    for sl, out in zip(subs, outs):
        o_ref[sl, :] = out.astype(o_ref.dtype)


def memory_attention(q, k, v, *, batch, seq, mem_len, tq=2048, n_sub=4):
    m = batch * seq
    nq = seq // tq
    return pl.pallas_call(
        functools.partial(_xattn_kernel, n_sub=n_sub),
        grid=(batch, XA_HEADS, nq),
        in_specs=[pl.BlockSpec((tq, XA_HEAD_DIM), lambda b, h, i: (b * nq + i, h)),
                  pl.BlockSpec((mem_len, XA_HEAD_DIM), lambda b, h, i: (b, h)),
                  pl.BlockSpec((mem_len, XA_HEAD_DIM), lambda b, h, i: (b, h))],
        out_specs=pl.BlockSpec((tq, XA_HEAD_DIM), lambda b, h, i: (b * nq + i, h)),
        out_shape=jax.ShapeDtypeStruct((m, D_MODEL), BF16),
        compiler_params=_params("parallel", "parallel", "arbitrary"),
        name="memory_attention",
    )(q, k, v)


def kernel(x, mem, pre_norm_g, w_in, conv_w, conv_b, conv_ln_g, conv_ln_b, ssm_lambda_re, ssm_lambda_im, ssm_log_dt, ssm_b_re, ssm_b_im, ssm_c_re, ssm_c_im, ssm_d, ssm_glu_w, ssm_glu_b, branch_norm_g, w_out, post_norm_g, xa_pre_g, xa_mem_g, xa_wq, xa_wk, xa_wv, xa_wo, xa_post_g):
    batch, seq, d = x.shape
    mem_len = mem.shape[1]
    depth = w_in.shape[0]
    xs = x.reshape(batch * seq, d)
    mems = mem.reshape(batch * mem_len, d)

    b_map, c_map, lam_bar = ssm_params(ssm_lambda_re, ssm_lambda_im, ssm_log_dt,
                                       ssm_b_re, ssm_b_im, ssm_c_re, ssm_c_im)
    glu_w = ssm_glu_w.astype(BF16)
    ga, gb, gc = (branch_norm_g[:, :CONV_WIDTH], branch_norm_g[:, CONV_WIDTH:CONV_WIDTH + SB_WIDTH],
                  branch_norm_g[:, CONV_WIDTH + SB_WIDTH:])

    h = rmsnorm_cast(xs, pre_norm_g[0])
    for l in range(depth):
        proj = matmul([h], w_in, l, tm=MM_ROWS, tn=MM_COLS, name="in_proj")
        ya = conformer_conv(proj, conv_w[l], conv_b[l], conv_ln_g[l], conv_ln_b[l], ga[l], batch=batch, seq=seq)
        yb = sb_gate_norm(stick_breaking_attention(proj, batch=batch, seq=seq), proj, gb[l])
        yc = s5_ssm(proj, b_map, c_map, lam_bar, l, ssm_d[l], glu_w[l], ssm_glu_b[l], gc[l],
                    batch=batch, seq=seq)
        o = matmul([ya, yb, yc], w_out, l, tm=MM_ROWS, tn=MM_COLS, name="out_proj")
        xs, h2 = residual_norm(o, xs, post_norm_g[l], xa_pre_g[l])

        mn = rmsnorm_cast(mems, xa_mem_g[l])
        q = matmul([h2], xa_wq, l, tm=MM_ROWS, tn=MM_COLS, name="xa_q")
        k = matmul([mn], xa_wk, l, tm=MM_ROWS, tn=MM_COLS, name="xa_k")
        v = matmul([mn], xa_wv, l, tm=MM_ROWS, tn=MM_COLS, name="xa_v")
        a = memory_attention(q, k, v, batch=batch, seq=seq, mem_len=mem_len)
        o = matmul([a], xa_wo, l, tm=MM_ROWS, tn=MM_COLS, name="xa_o")
        g_next = pre_norm_g[l + 1] if l + 1 < depth else None
        xs, h = residual_norm(o, xs, xa_post_g[l], g_next)
    return xs.reshape(batch, seq, d)
```
